```python
import math
import jax, jax.numpy as jnp
from jax import lax
import numpy as np

D_MODEL = 2048
BATCH = 2
SEQ = 16384
DEPTH = 1

POOL_WIDTH = D_MODEL // 2
POOL_WINDOWS = (2, 4, 8, 16)
N_POOL_GROUPS = len(POOL_WINDOWS)
POOL_GROUP = POOL_WIDTH // N_POOL_GROUPS
ATTN_WIDTH = D_MODEL - POOL_WIDTH
DIFF_HEAD_DIM = 64
N_DIFF_HEADS = ATTN_WIDTH // (2 * DIFF_HEAD_DIM)
DIFF_V_DIM = 2 * DIFF_HEAD_DIM
ROPE_DIM = DIFF_HEAD_DIM // 4
ROPE_THETA = 500000.0
Q_BLOCK = 128
IN_WIDTH = POOL_WIDTH + 3 * ATTN_WIDTH
N_GROUPS = 8
EXPERTS_PER_GROUP = 8
N_EXPERTS = N_GROUPS * EXPERTS_PER_GROUP
TOP_K_INNER = 2
D_EXPERT = D_MODEL // 4
MOE_BLOCK = 256
EPS = 1e-6

kernel_name = "hybrid_pool_diffattn_hmoe"


def rmsnorm(x, g):
    xf = x.astype(jnp.float32)
    y = xf * lax.rsqrt(jnp.mean(xf * xf, axis=-1, keepdims=True) + EPS)
    return (y * g.astype(jnp.float32)).astype(x.dtype)


def rope_tables(seq):
    inv = ROPE_THETA ** (-jnp.arange(0, ROPE_DIM, 2, dtype=jnp.float32) / ROPE_DIM)
    ang = jnp.arange(seq, dtype=jnp.float32)[:, None] * inv[None, :]
    ang = jnp.concatenate([ang, ang], axis=-1)
    return jnp.cos(ang), jnp.sin(ang)


def apply_partial_rope(t, cos, sin):
    rot = t[..., :ROPE_DIM].astype(jnp.float32)
    rest = t[..., ROPE_DIM:]
    half = ROPE_DIM // 2
    rotated = jnp.concatenate([-rot[..., half:], rot[..., :half]], axis=-1)
    c = cos[None, :, None, None, :]
    s = sin[None, :, None, None, :]
    rot = rot * c + rotated * s
    return jnp.concatenate([rot.astype(t.dtype), rest], axis=-1)


def pool_mixer(u, w_pool, pool_scale):
    B, S, _ = u.shape
    uf = u.astype(jnp.float32).reshape(B, S, N_POOL_GROUPS, POOL_GROUP)
    csum = jnp.cumsum(uf, axis=1)
    pos = jnp.arange(S)
    outs = []
    for g, w in enumerate(POOL_WINDOWS):
        c = csum[:, :, g]
        c_prev = jnp.pad(c, ((0, 0), (w, 0), (0, 0)))[:, :S]
        count = jnp.minimum(pos + 1, w).astype(jnp.float32)[None, :, None]
        outs.append((c - c_prev) / count - uf[:, :, g])
    pooled = jnp.stack(outs, axis=2).astype(u.dtype)
    mixed = jnp.einsum('bsgc,gcd->bsgd', pooled, w_pool)
    return mixed.reshape(B, S, POOL_WIDTH) * pool_scale


def diff_attention(q, k, v, lam, subln_g, lam_init):
    B, S, H = q.shape[0], q.shape[1], q.shape[2]
    n_blk = S // Q_BLOCK
    q_blocks = (q * (DIFF_HEAD_DIM ** -0.5)).reshape(B, n_blk, Q_BLOCK, H, 2, DIFF_HEAD_DIM)
    q_blocks = q_blocks.transpose(1, 0, 2, 3, 4, 5)
    k_pos = jnp.arange(S)

    def one_block(args):
        qb, start = args
        s = jnp.einsum('bqhcd,bkhcd->bhcqk', qb, k, preferred_element_type=jnp.float32)
        q_pos = start + jnp.arange(Q_BLOCK)
        mask = k_pos[None, :] <= q_pos[:, None]
        s = jnp.where(mask, s, -jnp.inf)
        p = jax.nn.softmax(s, axis=-1)
        a = p[:, :, 0] - lam * p[:, :, 1]
        return jnp.einsum('bhqk,bkhe->bqhe', a.astype(v.dtype), v)

    starts = jnp.arange(n_blk) * Q_BLOCK
    o = lax.map(one_block, (q_blocks, starts))
    o = o.transpose(1, 0, 2, 3, 4).reshape(B, S, H, DIFF_V_DIM)
    o = rmsnorm(o, subln_g) * (1.0 - lam_init)
    return o.reshape(B, S, ATTN_WIDTH)


def hier_moe(x, w_grp, b_grp, w_exp, b_exp, w_gate, w_up, w_down):
    B, S, D = x.shape
    N = B * S
    xt = x.reshape(N, D)
    grp_prob = jax.nn.softmax(jnp.matmul(xt, w_grp).astype(jnp.float32) + b_grp.astype(jnp.float32), axis=-1)
    grp_gate, grp_idx = lax.top_k(grp_prob, 1)
    exp_logits = jnp.einsum('nd,gde->nge', xt, w_exp).astype(jnp.float32) + b_exp.astype(jnp.float32)
    sel_logits = jnp.take_along_axis(exp_logits, grp_idx[:, :, None], axis=1)[:, 0]
    in_logits, in_idx = lax.top_k(sel_logits, TOP_K_INNER)
    gates = grp_gate * jax.nn.softmax(in_logits, axis=-1)
    expert_id = grp_idx * EXPERTS_PER_GROUP + in_idx

    A = N * TOP_K_INNER
    flat_e = expert_id.reshape(A).astype(jnp.int32)
    flat_tok = jnp.repeat(jnp.arange(N, dtype=jnp.int32), TOP_K_INNER)
    flat_g = gates.reshape(A)
    order = jnp.argsort(flat_e)
    e_sorted = flat_e[order]
    counts = jnp.bincount(flat_e, length=N_EXPERTS)
    starts = jnp.cumsum(counts) - counts
    padded = (counts + MOE_BLOCK - 1) // MOE_BLOCK * MOE_BLOCK
    pends = jnp.cumsum(padded)
    pstarts = pends - padded
    dest = pstarts[e_sorted] + (jnp.arange(A) - starts[e_sorted])
    n_blocks = -(-(A + N_EXPERTS * (MOE_BLOCK - 1)) // MOE_BLOCK)
    P = n_blocks * MOE_BLOCK
    row_tok = jnp.full((P,), N, jnp.int32).at[dest].set(flat_tok[order])
    row_gate = jnp.zeros((P,), jnp.float32).at[dest].set(flat_g[order])
    blk_start = jnp.arange(n_blocks) * MOE_BLOCK
    blk_expert = jnp.minimum(jnp.searchsorted(pends, blk_start, side='right'), N_EXPERTS - 1)

    x_pad = jnp.concatenate([xt, jnp.zeros((1, D), xt.dtype)], axis=0)
    xs = x_pad[row_tok].reshape(n_blocks, MOE_BLOCK, D)

    def expert_block(args):
        xb, e = args
        h = jax.nn.silu(xb @ w_gate[e]) * (xb @ w_up[e])
        return h @ w_down[e]

    ys = lax.map(expert_block, (xs, blk_expert)).reshape(P, D)
    ys = ys * row_gate[:, None].astype(ys.dtype)
    out = jax.ops.segment_sum(ys, row_tok, num_segments=N + 1)[:N]
    return out.reshape(B, S, D)


def setup_inputs(seed: int = 0) -> dict:
    key = jax.random.key(seed)
    ks = jax.random.split(key, 20)
    f32 = jnp.float32
    L = DEPTH
    nrm = lambda k, shape, scale: jax.random.normal(k, shape, f32) * scale
    return {
        "x": jax.random.normal(ks[0], (BATCH, SEQ, D_MODEL), f32),
        "norm_mix_g": 1.0 + nrm(ks[1], (L, D_MODEL), 0.02),
        "w_in": nrm(ks[2], (L, D_MODEL, IN_WIDTH), D_MODEL ** -0.5),
        "w_pool": nrm(ks[3], (L, N_POOL_GROUPS, POOL_GROUP, POOL_GROUP), POOL_GROUP ** -0.5),
        "pool_scale": 1.0 + nrm(ks[4], (L, POOL_WIDTH), 0.02),
        "lambda_q1": nrm(ks[5], (L, DIFF_HEAD_DIM), 0.1),
        "lambda_k1": nrm(ks[6], (L, DIFF_HEAD_DIM), 0.1),
        "lambda_q2": nrm(ks[7], (L, DIFF_HEAD_DIM), 0.1),
        "lambda_k2": nrm(ks[8], (L, DIFF_HEAD_DIM), 0.1),
        "subln_g": 1.0 + nrm(ks[9], (L, DIFF_V_DIM), 0.02),
        "w_out": nrm(ks[10], (L, D_MODEL, D_MODEL), D_MODEL ** -0.5),
        "norm_ffn_g": 1.0 + nrm(ks[11], (L, D_MODEL), 0.02),
        "w_grp": nrm(ks[12], (L, D_MODEL, N_GROUPS), D_MODEL ** -0.5),
        "b_grp": nrm(ks[13], (L, N_GROUPS), 0.01),
        "w_exp": nrm(ks[14], (L, N_GROUPS, D_MODEL, EXPERTS_PER_GROUP), D_MODEL ** -0.5),
        "b_exp": nrm(ks[15], (L, N_GROUPS, EXPERTS_PER_GROUP), 0.01),
        "w_gate": nrm(ks[16], (L, N_EXPERTS, D_MODEL, D_EXPERT), D_MODEL ** -0.5),
        "w_up": nrm(ks[17], (L, N_EXPERTS, D_MODEL, D_EXPERT), D_MODEL ** -0.5),
        "w_down": nrm(ks[18], (L, N_EXPERTS, D_EXPERT, D_MODEL), D_EXPERT ** -0.5),
        "norm_final_g": 1.0 + nrm(ks[19], (D_MODEL,), 0.02),
    }


def reference(x, norm_mix_g, w_in, w_pool, pool_scale, lambda_q1, lambda_k1, lambda_q2, lambda_k2,
              subln_g, w_out, norm_ffn_g, w_grp, b_grp, w_exp, b_exp, w_gate, w_up, w_down,
              norm_final_g):
    B, S, _ = x.shape
    H, d = N_DIFF_HEADS, DIFF_HEAD_DIM
    cos, sin = rope_tables(S)
    h = x
    for l in range(DEPTH):
        lam_init = 0.8 - 0.6 * math.exp(-0.3 * l)
        u = rmsnorm(h, norm_mix_g[l])
        proj = jnp.matmul(u, w_in[l])
        pool_in = proj[..., :POOL_WIDTH]
        q = proj[..., POOL_WIDTH:POOL_WIDTH + ATTN_WIDTH].reshape(B, S, H, 2, d)
        k = proj[..., POOL_WIDTH + ATTN_WIDTH:POOL_WIDTH + 2 * ATTN_WIDTH].reshape(B, S, H, 2, d)
        v = proj[..., POOL_WIDTH + 2 * ATTN_WIDTH:].reshape(B, S, H, DIFF_V_DIM)
        q = apply_partial_rope(q, cos, sin)
        k = apply_partial_rope(k, cos, sin)
        lq1 = lambda_q1[l].astype(jnp.float32)
        lk1 = lambda_k1[l].astype(jnp.float32)
        lq2 = lambda_q2[l].astype(jnp.float32)
        lk2 = lambda_k2[l].astype(jnp.float32)
        lam = jnp.exp(jnp.sum(lq1 * lk1)) - jnp.exp(jnp.sum(lq2 * lk2)) + lam_init
        pool_out = pool_mixer(pool_in, w_pool[l], pool_scale[l])
        attn_out = diff_attention(q, k, v, lam, subln_g[l], lam_init)
        mixed = jnp.concatenate([pool_out.astype(h.dtype), attn_out.astype(h.dtype)], axis=-1)
        h = h + jnp.matmul(mixed, w_out[l])
        h = h + hier_moe(rmsnorm(h, norm_ffn_g[l]), w_grp[l], b_grp[l], w_exp[l], b_exp[l],
                         w_gate[l], w_up[l], w_down[l])
    return rmsnorm(h, norm_final_g)
```

```python
import functools
import math

import jax
import jax.numpy as jnp
from jax import lax
from jax.experimental import pallas as pl
from jax.experimental.pallas import tpu as pltpu

F32 = jnp.float32
BF16 = jnp.bfloat16

EPS = 1e-6
POOL_WINDOWS = (2, 4, 8, 16)
POOL_HALO = 16
HEAD_DIM = 64
HEAD_WIDTH = 2 * HEAD_DIM
ROPE_DIM = HEAD_DIM // 4
ROPE_THETA = 500000.0
TOP_K_INNER = 2
MASK_VALUE = -1e30
LANES = 128
ROUTER_LANES = 128
VMEM_LIMIT = 56 * 1024 * 1024


def _rms(x, g):
    ms = jnp.mean(x * x, axis=-1, keepdims=True)
    return x * lax.rsqrt(ms + EPS) * g


def _inproj_kernel(x_ref, g_ref, w_ref, cs_ref, sn_ref, o_ref, u_ref, *, first_rope_blk, n_rope_blk):
    j = pl.program_id(1)

    @pl.when(j == 0)
    def _():
        u_ref[...] = _rms(x_ref[...], g_ref[...]).astype(BF16)

    acc = jnp.dot(u_ref[...], w_ref[...], preferred_element_type=F32)
    is_rope = (j >= first_rope_blk) & (j < first_rope_blk + 2 * n_rope_blk)

    @pl.when(is_rope)
    def _():
        scale = jnp.where(j < first_rope_blk + n_rope_blk, HEAD_DIM ** -0.5, 1.0).astype(F32)
        cs = cs_ref[...]
        sn = sn_ref[...]
        lane = lax.broadcasted_iota(jnp.int32, cs.shape, 1)
        first_half = (lane % HEAD_DIM) < (ROPE_DIM // 2)
        for c in range(acc.shape[1] // LANES):
            t = acc[:, c * LANES:(c + 1) * LANES]
            partner = jnp.where(first_half,
                                pltpu.roll(t, LANES - ROPE_DIM // 2, 1),
                                pltpu.roll(t, ROPE_DIM // 2, 1))
            o_ref[:, c * LANES:(c + 1) * LANES] = ((t * cs + partner * sn) * scale).astype(BF16)

    @pl.when(jnp.logical_not(is_rope))
    def _():
        o_ref[...] = acc.astype(BF16)


def _inproj(x2, g, w_bf16, cs, sn, *, seq, pool_width, attn_width, tm, tn):
    n, d = x2.shape
    width = w_bf16.shape[1]
    assert n % tm == 0 and seq % tm == 0 and pool_width % tn == 0 and attn_width % tn == 0
    kern = functools.partial(_inproj_kernel, first_rope_blk=pool_width // tn, n_rope_blk=attn_width // tn)
    seq_blocks = seq // tm
    return pl.pallas_call(
        kern,
        grid=(n // tm, width // tn),
        in_specs=[
            pl.BlockSpec((tm, d), lambda i, j: (i, 0)),
            pl.BlockSpec((1, d), lambda i, j: (0, 0)),
            pl.BlockSpec((d, tn), lambda i, j: (0, j)),
            pl.BlockSpec((tm, LANES), lambda i, j: (i % seq_blocks, 0)),
            pl.BlockSpec((tm, LANES), lambda i, j: (i % seq_blocks, 0)),
        ],
        out_specs=pl.BlockSpec((tm, tn), lambda i, j: (i, j)),
        out_shape=jax.ShapeDtypeStruct((n, width), BF16),
        scratch_shapes=[pltpu.VMEM((tm, d), BF16)],
        compiler_params=pltpu.CompilerParams(
            dimension_semantics=("arbitrary", "arbitrary"), vmem_limit_bytes=VMEM_LIMIT),
        name="inproj",
    )(x2, g, w_bf16, cs, sn)


def _rope_tables(seq):
    half = ROPE_DIM // 2
    inv = ROPE_THETA ** (-jnp.arange(0, ROPE_DIM, 2, dtype=F32) / ROPE_DIM)
    ang = jnp.arange(seq, dtype=F32)[:, None] * inv[None, :]
    cos, sin = jnp.cos(ang), jnp.sin(ang)
    rest = HEAD_DIM - ROPE_DIM
    cs = jnp.concatenate([cos, cos, jnp.ones((seq, rest), F32)], axis=1)
    sn = jnp.concatenate([-sin, sin, jnp.zeros((seq, rest), F32)], axis=1)
    reps = LANES // HEAD_DIM
    return jnp.tile(cs, (1, reps)), jnp.tile(sn, (1, reps))


def _attn_kernel(lam_ref, g_ref, q_ref, k_ref, v_ref, o_ref, m_ref, l_ref, acc_ref, *, tq, tk, lam_init):
    qi = pl.program_id(2)
    q = q_ref[...]
    lane = lax.broadcasted_iota(jnp.int32, q.shape, 1)
    zero = jnp.zeros_like(q)
    qs = (jnp.where(lane < HEAD_DIM, q, zero), jnp.where(lane >= HEAD_DIM, q, zero))

    m_ref[...] = jnp.full(m_ref.shape, MASK_VALUE, F32)
    l_ref[...] = jnp.zeros(l_ref.shape, F32)
    acc_ref[...] = jnp.zeros(acc_ref.shape, F32)

    def tile(kb, masked):
        start = pl.multiple_of(kb * tk, tk)
        k = k_ref[pl.ds(start, tk), :]
        v = v_ref[pl.ds(start, tk), :]
        if masked:
            row = qi * tq + lax.broadcasted_iota(jnp.int32, (tq, tk), 0)
            col = start + lax.broadcasted_iota(jnp.int32, (tq, tk), 1)
            keep = col <= row
        for c in range(2):
            s = lax.dot_general(qs[c], k, (((1,), (1,)), ((), ())), preferred_element_type=F32)
            if masked:
                s = jnp.where(keep, s, MASK_VALUE)
            m_prev = m_ref[c]
            m_new = jnp.maximum(m_prev, jnp.max(s, axis=1, keepdims=True))
            alpha = jnp.exp(m_prev - m_new)
            p = jnp.exp(s - m_new[:, :1])
            l_ref[c] = alpha * l_ref[c] + jnp.sum(p, axis=1, keepdims=True)
            acc_ref[c] = acc_ref[c] * alpha + jnp.dot(p.astype(BF16), v, preferred_element_type=F32)
            m_ref[c] = m_new

    n_full = (qi * tq) // tk

    def body(kb, carry):
        tile(kb, False)
        return carry

    lax.fori_loop(0, n_full, body, 0)
    for d in range(tq // tk):
        tile(n_full + d, True)

    lp = lam_ref[...]
    lam = (jnp.exp(jnp.sum(lp[0:1] * lp[1:2], axis=1, keepdims=True))
           - jnp.exp(jnp.sum(lp[2:3] * lp[3:4], axis=1, keepdims=True)) + lam_init)
    o = acc_ref[0] / l_ref[0] - lam * (acc_ref[1] / l_ref[1])
    o_ref[...] = (_rms(o, g_ref[...]) * (1.0 - lam_init)).astype(BF16)


def _attention(proj, lam_params, subln_g, *, batch, seq, n_heads, q_col, k_col, v_col, tq, tk, lam_init):
    n = batch * seq
    assert seq % tq == 0 and tq % tk == 0
    nq = seq // tq
    kern = functools.partial(_attn_kernel, tq=tq, tk=tk, lam_init=lam_init)
    return pl.pallas_call(
        kern,
        grid=(batch, n_heads, nq),
        in_specs=[
            pl.BlockSpec(lam_params.shape, lambda b, h, i: (0, 0)),
            pl.BlockSpec((1, HEAD_WIDTH), lambda b, h, i: (0, 0)),
            pl.BlockSpec((tq, HEAD_WIDTH), lambda b, h, i: (b * nq + i, q_col + h)),
            pl.BlockSpec((seq, HEAD_WIDTH), lambda b, h, i: (b, k_col + h)),
            pl.BlockSpec((seq, HEAD_WIDTH), lambda b, h, i: (b, v_col + h)),
        ],
        out_specs=pl.BlockSpec((tq, HEAD_WIDTH), lambda b, h, i: (b * nq + i, h)),
        out_shape=jax.ShapeDtypeStruct((n, n_heads * HEAD_WIDTH), BF16),
        scratch_shapes=[
            pltpu.VMEM((2, tq, LANES), F32),
            pltpu.VMEM((2, tq, LANES), F32),
            pltpu.VMEM((2, tq, HEAD_WIDTH), F32),
        ],
        compiler_params=pltpu.CompilerParams(
            dimension_semantics=("arbitrary", "arbitrary", "arbitrary"), vmem_limit_bytes=VMEM_LIMIT),
        name="diff_attention",
    )(lam_params, subln_g, proj, proj, proj)


def _post_kernel(pool_ref, halo_ref, attn_ref, x_ref, wpool_ref, pscale_ref, wout_ref, gffn_ref,
                 wrt_ref, brt_ref, h_ref, xn_ref, meta_ref, *, tm, seq_blocks, n_groups, per_group):
    i = pl.program_id(0)
    blk_in_seq = i % seq_blocks
    halo = halo_ref[...].astype(F32) * jnp.where(blk_in_seq == 0, 0.0, 1.0).astype(F32)
    cur = pool_ref[...].astype(F32)
    ext = jnp.concatenate([halo, cur], axis=0)
    pos = blk_in_seq * tm + lax.broadcasted_iota(jnp.int32, (tm, 1), 0)
    pool_width = cur.shape[1]
    group = pool_width // len(POOL_WINDOWS)

    h = x_ref[...]
    for g, w in enumerate(POOL_WINDOWS):
        sl = slice(g * group, (g + 1) * group)
        s = ext[:, sl]
        span = 1
        while span < w:
            s = s + pltpu.roll(s, span, 0)
            span *= 2
        count = jnp.minimum(pos + 1, w).astype(F32)
        pooled = s[POOL_HALO:, :] / count - cur[:, sl]
        mixed = jnp.dot(pooled.astype(BF16), wpool_ref[g], preferred_element_type=F32) * pscale_ref[:, sl]
        h = h + jnp.dot(mixed.astype(BF16), wout_ref[sl, :], preferred_element_type=F32)
    h = h + jnp.dot(attn_ref[...], wout_ref[pool_width:, :], preferred_element_type=F32)
    h_ref[...] = h

    xn = _rms(h, gffn_ref[...])
    xn_ref[...] = xn
    logits = jnp.dot(xn.astype(BF16), wrt_ref[...], preferred_element_type=F32) + brt_ref[...]

    lane = lax.broadcasted_iota(jnp.int32, logits.shape, 1).astype(F32)
    big = float(ROUTER_LANES)
    is_grp = lane < n_groups
    gl = jnp.where(is_grp, logits, MASK_VALUE)
    gmax = jnp.max(gl, axis=1, keepdims=True)
    gsum = jnp.sum(jnp.where(is_grp, jnp.exp(gl - gmax), 0.0), axis=1, keepdims=True)
    grp_gate = 1.0 / gsum
    gidx = jnp.min(jnp.where(gl == gmax, lane, big), axis=1, keepdims=True)
    lo = n_groups + gidx * per_group
    in_sel = (lane >= lo) & (lane < lo + per_group)
    el = jnp.where(in_sel, logits, MASK_VALUE)
    e1 = jnp.max(el, axis=1, keepdims=True)
    i1 = jnp.min(jnp.where(el == e1, lane, big), axis=1, keepdims=True)
    el2 = jnp.where(lane == i1, MASK_VALUE, el)
    e2 = jnp.max(el2, axis=1, keepdims=True)
    i2 = jnp.min(jnp.where(el2 == e2, lane, big), axis=1, keepdims=True)
    t = jnp.exp(e2 - e1)
    g1 = grp_gate / (1.0 + t)
    g2 = g1 * t
    meta = jnp.where(lane == 0, i1 - n_groups,
                     jnp.where(lane == 1, i2 - n_groups,
                               jnp.where(lane == 2, g1, jnp.where(lane == 3, g2, 0.0))))
    meta_ref[...] = meta


def _post(proj, attn, x2, w_pool, pool_scale, w_out, g_ffn, w_rt, b_rt, *, seq, pool_width, tm,
          n_groups, per_group):
    n, d = x2.shape
    assert n % tm == 0 and seq % tm == 0 and tm % POOL_HALO == 0
    seq_blocks = seq // tm
    halo_per_blk = tm // POOL_HALO
    kern = functools.partial(_post_kernel, tm=tm, seq_blocks=seq_blocks, n_groups=n_groups, per_group=per_group)
    const2 = lambda i: (0, 0)
    return pl.pallas_call(
        kern,
        grid=(n // tm,),
        in_specs=[
            pl.BlockSpec((tm, pool_width), lambda i: (i, 0)),
            pl.BlockSpec((POOL_HALO, pool_width), lambda i: (jnp.maximum(i * halo_per_blk - 1, 0), 0)),
            pl.BlockSpec((tm, attn.shape[1]), lambda i: (i, 0)),
            pl.BlockSpec((tm, d), lambda i: (i, 0)),
            pl.BlockSpec(w_pool.shape, lambda i: (0, 0, 0)),
            pl.BlockSpec(pool_scale.shape, const2),
            pl.BlockSpec(w_out.shape, const2),
            pl.BlockSpec(g_ffn.shape, const2),
            pl.BlockSpec(w_rt.shape, const2),
            pl.BlockSpec(b_rt.shape, const2),
        ],
        out_specs=[
            pl.BlockSpec((tm, d), lambda i: (i, 0)),
            pl.BlockSpec((tm, d), lambda i: (i, 0)),
            pl.BlockSpec((tm, ROUTER_LANES), lambda i: (i, 0)),
        ],
        out_shape=[
            jax.ShapeDtypeStruct((n, d), F32),
            jax.ShapeDtypeStruct((n, d), F32),
            jax.ShapeDtypeStruct((n, ROUTER_LANES), F32),
        ],
        compiler_params=pltpu.CompilerParams(
            dimension_semantics=("arbitrary",), vmem_limit_bytes=VMEM_LIMIT),
        name="post_mix_router",
    )(proj, proj, attn, x2, w_pool, pool_scale, w_out, g_ffn, w_rt, b_rt)


def _moe_kernel(be_ref, nused_ref, rtok_ref, xn_hbm, wg_ref, wu_ref, wd_ref, y_ref,
                xbuf, wg_bf, wu_bf, wd_bf, sem, *, tb):
    b = pl.program_id(0)
    nused = nused_ref[0]

    def row_copy(blk, r, slot):
        tok = rtok_ref[blk * tb + r]
        return pltpu.make_async_copy(xn_hbm.at[pl.ds(tok, 1)], xbuf.at[slot, pl.ds(r, 1)], sem.at[slot])

    def issue(blk, slot):
        def body(r, carry):
            row_copy(blk, r, slot).start()
            return carry
        lax.fori_loop(0, tb, body, 0)

    @pl.when(b == 0)
    def _():
        issue(0, 0)

    @pl.when(b + 1 < nused)
    def _():
        issue(b + 1, (b + 1) % 2)

    @pl.when(b >= nused)
    def _():
        y_ref[...] = jnp.zeros(y_ref.shape, F32)

    @pl.when(b < nused)
    def _():
        slot = b % 2

        def wait_body(r, carry):
            row_copy(b, r, slot).wait()
            return carry
        lax.fori_loop(0, tb, wait_body, 0)

        new_expert = (b == 0) | (be_ref[b] != be_ref[jnp.maximum(b - 1, 0)])

        @pl.when(new_expert)
        def _():
            wg_bf[...] = wg_ref[0].astype(BF16)
            wu_bf[...] = wu_ref[0].astype(BF16)
            wd_bf[...] = wd_ref[0].astype(BF16)

        xb = xbuf[slot].astype(BF16)
        hg = jnp.dot(xb, wg_bf[...], preferred_element_type=F32)
        hu = jnp.dot(xb, wu_bf[...], preferred_element_type=F32)
        act = hg * jax.nn.sigmoid(hg) * hu
        y_ref[...] = jnp.dot(act.astype(BF16), wd_bf[...], preferred_element_type=F32)


def _moe(blk_expert, n_used, row_tok, xn, w_gate, w_up, w_down, *, tb):
    n, d = xn.shape
    n_blocks = blk_expert.shape[0]
    de = w_gate.shape[2]
    kern = functools.partial(_moe_kernel, tb=tb)
    grid_spec = pltpu.PrefetchScalarGridSpec(
        num_scalar_prefetch=3,
        grid=(n_blocks,),
        in_specs=[
            pl.BlockSpec(memory_space=pl.ANY),
            pl.BlockSpec((1, d, de), lambda b, be, nu, rt: (be[b], 0, 0)),
            pl.BlockSpec((1, d, de), lambda b, be, nu, rt: (be[b], 0, 0)),
            pl.BlockSpec((1, de, d), lambda b, be, nu, rt: (be[b], 0, 0)),
        ],
        out_specs=pl.BlockSpec((tb, d), lambda b, be, nu, rt: (b, 0)),
        scratch_shapes=[
            pltpu.VMEM((2, tb, d), F32),
            pltpu.VMEM((d, de), BF16),
            pltpu.VMEM((d, de), BF16),
            pltpu.VMEM((de, d), BF16),
            pltpu.SemaphoreType.DMA((2,)),
        ],
    )
    return pl.pallas_call(
        kern,
        grid_spec=grid_spec,
        out_shape=jax.ShapeDtypeStruct((n_blocks * tb, d), F32),
        compiler_params=pltpu.CompilerParams(
            dimension_semantics=("arbitrary",), vmem_limit_bytes=VMEM_LIMIT),
        name="moe_experts",
    )(blk_expert, n_used, row_tok, xn, w_gate, w_up, w_down)


def _combine_kernel(pos_ref, h_ref, meta_ref, g_ref, ys_hbm, o_ref, ybuf, sem, *, tm, final_norm):
    i = pl.program_id(0)
    nsteps = pl.num_programs(0)

    def row_copy(blk, r, k, slot):
        p = pos_ref[(blk * tm + r) * TOP_K_INNER + k]
        return pltpu.make_async_copy(ys_hbm.at[pl.ds(p, 1)], ybuf.at[slot, k, pl.ds(r, 1)], sem.at[slot])

    def issue(blk, slot):
        def body(r, carry):
            for k in range(TOP_K_INNER):
                row_copy(blk, r, k, slot).start()
            return carry
        lax.fori_loop(0, tm, body, 0)

    @pl.when(i == 0)
    def _():
        issue(0, 0)

    @pl.when(i + 1 < nsteps)
    def _():
        issue(i + 1, (i + 1) % 2)

    slot = i % 2

    def wait_body(r, carry):
        for k in range(TOP_K_INNER):
            row_copy(i, r, k, slot).wait()
        return carry
    lax.fori_loop(0, tm, wait_body, 0)

    meta = meta_ref[...]
    h = h_ref[...]
    for k in range(TOP_K_INNER):
        h = h + meta[:, TOP_K_INNER + k:TOP_K_INNER + k + 1] * ybuf[slot, k]
    o_ref[...] = _rms(h, g_ref[...]) if final_norm else h


def _combine(pos, h, meta, g_final, ys, *, tm, final_norm):
    n, d = h.shape
    assert n % tm == 0
    kern = functools.partial(_combine_kernel, tm=tm, final_norm=final_norm)
    grid_spec = pltpu.PrefetchScalarGridSpec(
        num_scalar_prefetch=1,
        grid=(n // tm,),
        in_specs=[
            pl.BlockSpec((tm, d), lambda i, p: (i, 0)),
            pl.BlockSpec((tm, ROUTER_LANES), lambda i, p: (i, 0)),
            pl.BlockSpec((1, d), lambda i, p: (0, 0)),
            pl.BlockSpec(memory_space=pl.ANY),
        ],
        out_specs=pl.BlockSpec((tm, d), lambda i, p: (i, 0)),
        scratch_shapes=[
            pltpu.VMEM((2, TOP_K_INNER, tm, d), F32),
            pltpu.SemaphoreType.DMA((2,)),
        ],
    )
    return pl.pallas_call(
        kern,
        grid_spec=grid_spec,
        out_shape=jax.ShapeDtypeStruct((n, d), F32),
        compiler_params=pltpu.CompilerParams(
            dimension_semantics=("arbitrary",), vmem_limit_bytes=VMEM_LIMIT),
        name="combine",
    )(pos, h, meta, g_final, ys)


def _dispatch_plan(expert_id, n_experts, tb):
    n = expert_id.shape[0]
    a = n * TOP_K_INNER
    flat_e = expert_id.reshape(a)
    order = jnp.argsort(flat_e)
    e_sorted = flat_e[order]
    counts = jnp.bincount(flat_e, length=n_experts)
    starts = jnp.cumsum(counts) - counts
    padded = (counts + tb - 1) // tb * tb
    pends = jnp.cumsum(padded)
    pstarts = pends - padded
    dest = (pstarts[e_sorted] + (jnp.arange(a) - starts[e_sorted])).astype(jnp.int32)
    n_blocks = -(-(a + n_experts * (tb - 1)) // tb)
    row_tok = jnp.zeros((n_blocks * tb,), jnp.int32).at[dest].set((order // TOP_K_INNER).astype(jnp.int32))
    pos = jnp.zeros((a,), jnp.int32).at[order].set(dest)
    blk_start = jnp.arange(n_blocks) * tb
    blk_expert = jnp.minimum(jnp.searchsorted(pends, blk_start, side="right"), n_experts - 1).astype(jnp.int32)
    n_used = (pends[-1] // tb).astype(jnp.int32).reshape(1)
    return blk_expert, n_used, row_tok, pos


def _pick(limit, total):
    t = min(limit, total)
    assert total % t == 0
    return t


def kernel(x, norm_mix_g, w_in, w_pool, pool_scale, lambda_q1, lambda_k1, lambda_q2, lambda_k2, subln_g, w_out, norm_ffn_g, w_grp, b_grp, w_exp, b_exp, w_gate, w_up, w_down, norm_final_g):
    batch, seq, d = x.shape
    n = batch * seq
    depth = w_in.shape[0]
    pool_width = w_pool.shape[1] * w_pool.shape[2]
    attn_width = (w_in.shape[2] - pool_width) // 3
    n_heads = attn_width // HEAD_WIDTH
    n_groups, per_group = w_exp.shape[1], w_exp.shape[3]
    n_experts = n_groups * per_group
    assert n_groups + n_experts <= ROUTER_LANES

    tm_proj = _pick(512, seq)
    tn_proj = _pick(1024, math.gcd(pool_width, attn_width))
    tq = _pick(512, seq)
    tm_post = _pick(256, seq)
    tb = 256
    tm_comb = _pick(256, n)

    cs, sn = _rope_tables(seq)
    h = x.reshape(n, d)
    for l in range(depth):
        lam_init = 0.8 - 0.6 * math.exp(-0.3 * l)
        proj = _inproj(h, norm_mix_g[l][None, :], w_in[l].astype(BF16), cs, sn, seq=seq,
                       pool_width=pool_width, attn_width=attn_width, tm=tm_proj, tn=tn_proj)
        lam_params = jnp.stack([lambda_q1[l], lambda_k1[l], lambda_q2[l], lambda_k2[l]]).astype(F32)
        col = lambda width: width // HEAD_WIDTH
        attn = _attention(proj, lam_params, subln_g[l][None, :].astype(F32), batch=batch, seq=seq,
                          n_heads=n_heads, q_col=col(pool_width), k_col=col(pool_width + attn_width),
                          v_col=col(pool_width + 2 * attn_width), tq=tq, tk=tq, lam_init=lam_init)

        w_rt = jnp.concatenate([w_grp[l], jnp.transpose(w_exp[l], (1, 0, 2)).reshape(d, n_experts)], axis=1)
        w_rt = jnp.pad(w_rt, ((0, 0), (0, ROUTER_LANES - w_rt.shape[1]))).astype(BF16)
        b_rt = jnp.concatenate([b_grp[l], b_exp[l].reshape(n_experts)]).astype(F32)
        b_rt = jnp.pad(b_rt, (0, ROUTER_LANES - b_rt.shape[0]))[None, :]
        h1, xn, meta = _post(proj, attn, h, w_pool[l].astype(BF16), pool_scale[l][None, :].astype(F32),
                             w_out[l].astype(BF16), norm_ffn_g[l][None, :].astype(F32), w_rt, b_rt,
                             seq=seq, pool_width=pool_width, tm=tm_post, n_groups=n_groups,
                             per_group=per_group)

        expert_id = meta[:, :TOP_K_INNER].astype(jnp.int32)
        blk_expert, n_used, row_tok, pos = _dispatch_plan(expert_id, n_experts, tb)
        ys = _moe(blk_expert, n_used, row_tok, xn, w_gate[l], w_up[l], w_down[l], tb=tb)
        h = _combine(pos, h1, meta, norm_final_g[None, :].astype(F32), ys, tm=tm_comb,
                     final_norm=(l == depth - 1))
    return h.reshape(batch, seq, d)
```

```python
import functools
import math

import jax
import jax.numpy as jnp
from jax import lax
from jax.experimental import pallas as pl
from jax.experimental.pallas import tpu as pltpu

F32 = jnp.float32
BF16 = jnp.bfloat16

EPS = 1e-6
POOL_WINDOWS = (2, 4, 8, 16)
POOL_HALO = 16
HEAD_DIM = 64
HEAD_WIDTH = 2 * HEAD_DIM
ROPE_DIM = HEAD_DIM // 4
ROPE_THETA = 500000.0
Q_SCALE = HEAD_DIM ** -0.5 * math.log2(math.e)
TOP_K_INNER = 2
MASK_VALUE = -1e30
LANES = 128
ROUTER_LANES = 128
ONES_ROWS = 16
ISSUE_UNROLL = 8
RANK_BLOCK = 256
VMEM_LIMIT = 56 * 1024 * 1024


def _rms(x, g):
    ms = jnp.mean(x * x, axis=-1, keepdims=True)
    return x * lax.rsqrt(ms + EPS) * g


def _inproj_kernel(x_ref, g_ref, w_ref, wvt_ref, cs_ref, sn_ref, o_ref, vt_ref, u_ref, *,
                   first_rope_blk, n_rope_blk, n_row_blk):
    j = pl.program_id(1)

    @pl.when(j == 0)
    def _():
        u_ref[...] = _rms(x_ref[...], g_ref[...]).astype(BF16)

    is_rope = (j >= first_rope_blk) & (j < first_rope_blk + 2 * n_rope_blk)

    @pl.when(j < n_row_blk)
    def _():
        acc = jnp.dot(u_ref[...], w_ref[...], preferred_element_type=F32)

        @pl.when(is_rope)
        def _():
            scale = jnp.where(j < first_rope_blk + n_rope_blk, Q_SCALE, 1.0).astype(F32)
            cs = cs_ref[...]
            sn = sn_ref[...]
            lane = lax.broadcasted_iota(jnp.int32, cs.shape, 1)
            first_half = (lane % HEAD_DIM) < (ROPE_DIM // 2)
            for c in range(acc.shape[1] // LANES):
                t = acc[:, c * LANES:(c + 1) * LANES]
                partner = jnp.where(first_half,
                                    pltpu.roll(t, LANES - ROPE_DIM // 2, 1),
                                    pltpu.roll(t, ROPE_DIM // 2, 1))
                o_ref[:, c * LANES:(c + 1) * LANES] = ((t * cs + partner * sn) * scale).astype(BF16)

        @pl.when(jnp.logical_not(is_rope))
        def _():
            o_ref[...] = acc.astype(BF16)

    @pl.when(j == n_row_blk)
    def _():
        vt = lax.dot_general(wvt_ref[...], u_ref[...], (((1,), (1,)), ((), ())), preferred_element_type=F32)
        vt_ref[...] = vt.astype(BF16)


def _inproj(x2, g, w_rows, w_vt, cs, sn, *, seq, pool_width, attn_width, tm, tn):
    n, d = x2.shape
    width = w_rows.shape[1]
    assert n % tm == 0 and seq % tm == 0 and pool_width % tn == 0 and attn_width % tn == 0
    n_row_blk = width // tn
    kern = functools.partial(_inproj_kernel, first_rope_blk=pool_width // tn, n_rope_blk=attn_width // tn,
                             n_row_blk=n_row_blk)
    seq_blocks = seq // tm
    last = n_row_blk - 1
    return pl.pallas_call(
        kern,
        grid=(n // tm, n_row_blk + 1),
        in_specs=[
            pl.BlockSpec((tm, d), lambda i, j: (i, 0)),
            pl.BlockSpec((1, d), lambda i, j: (0, 0)),
            pl.BlockSpec((d, tn), lambda i, j: (0, jnp.minimum(j, last))),
            pl.BlockSpec(w_vt.shape, lambda i, j: (0, 0)),
            pl.BlockSpec((tm, LANES), lambda i, j: (i % seq_blocks, 0)),
            pl.BlockSpec((tm, LANES), lambda i, j: (i % seq_blocks, 0)),
        ],
        out_specs=[
            pl.BlockSpec((tm, tn), lambda i, j: (i, jnp.minimum(j, last))),
            pl.BlockSpec((attn_width, tm), lambda i, j: (0, i)),
        ],
        out_shape=[
            jax.ShapeDtypeStruct((n, width), BF16),
            jax.ShapeDtypeStruct((attn_width, n), BF16),
        ],
        scratch_shapes=[pltpu.VMEM((tm, d), BF16)],
        compiler_params=pltpu.CompilerParams(
            dimension_semantics=("arbitrary", "arbitrary"), vmem_limit_bytes=VMEM_LIMIT),
        name="inproj",
    )(x2, g, w_rows, w_vt, cs, sn)


def _rope_tables(seq):
    half = ROPE_DIM // 2
    inv = ROPE_THETA ** (-jnp.arange(0, ROPE_DIM, 2, dtype=F32) / ROPE_DIM)
    ang = jnp.arange(seq, dtype=F32)[:, None] * inv[None, :]
    cos, sin = jnp.cos(ang), jnp.sin(ang)
    rest = HEAD_DIM - ROPE_DIM
    cs = jnp.concatenate([cos, cos, jnp.ones((seq, rest), F32)], axis=1)
    sn = jnp.concatenate([-sin, sin, jnp.zeros((seq, rest), F32)], axis=1)
    reps = LANES // HEAD_DIM
    return jnp.tile(cs, (1, reps)), jnp.tile(sn, (1, reps))


def _attn_kernel(lam_ref, g_ref, q_ref, k_ref, vt_ref, o_ref, m_ref, acc_ref, *, tq, tk, lam_init):
    qi = pl.program_id(2)
    q = q_ref[...]
    lane = lax.broadcasted_iota(jnp.int32, q.shape, 1)
    zero = jnp.zeros_like(q)
    qs = (jnp.where(lane < HEAD_DIM, q, zero), jnp.where(lane >= HEAD_DIM, q, zero))

    m_ref[...] = jnp.full(m_ref.shape, MASK_VALUE, F32)
    acc_ref[...] = jnp.zeros(acc_ref.shape, F32)

    def tile(start, width, masked):
        k = k_ref[pl.ds(start, width), :]
        vt = jnp.concatenate([vt_ref[:, pl.ds(start, width)], jnp.ones((ONES_ROWS, width), BF16)], axis=0)
        if masked:
            key = start + lax.broadcasted_iota(jnp.int32, (width, tq), 0)
            qry = qi * tq + lax.broadcasted_iota(jnp.int32, (width, tq), 1)
            keep = key <= qry
        for c in range(2):
            s = lax.dot_general(k, qs[c], (((1,), (1,)), ((), ())), preferred_element_type=F32)
            if masked:
                s = jnp.where(keep, s, MASK_VALUE)
            m_prev = m_ref[c]
            m_new = jnp.maximum(m_prev, jnp.max(s, axis=0, keepdims=True))
            alpha = jnp.exp2(m_prev - m_new)
            p = jnp.exp2(s - m_new).astype(BF16)
            acc_ref[c] = acc_ref[c] * alpha + jnp.dot(vt, p, preferred_element_type=F32)
            m_ref[c] = m_new

    q_start = qi * tq
    n_wide = q_start // tk

    def body(kb, carry):
        tile(pl.multiple_of(kb * tk, tk), tk, False)
        return carry

    lax.fori_loop(0, n_wide, body, 0)
    for r in range(tk // tq - 1):
        @pl.when(q_start - n_wide * tk > r * tq)
        def _():
            tile(pl.multiple_of(n_wide * tk + r * tq, tq), tq, False)
    tile(pl.multiple_of(q_start, tq), tq, True)

    lp = lam_ref[...]
    lam = (jnp.exp(jnp.sum(lp[0:1] * lp[1:2], axis=1, keepdims=True))
           - jnp.exp(jnp.sum(lp[2:3] * lp[3:4], axis=1, keepdims=True)) + lam_init)
    num = [acc_ref[c][:HEAD_WIDTH, :] for c in range(2)]
    den = [acc_ref[c][HEAD_WIDTH:HEAD_WIDTH + 1, :] for c in range(2)]
    ot = num[0] / den[0] - lam * (num[1] / den[1])
    ms = jnp.mean(ot * ot, axis=0, keepdims=True)
    ot = ot * lax.rsqrt(ms + EPS) * (g_ref[...] * (1.0 - lam_init))
    o_ref[...] = ot.T.astype(BF16)


def _attention(proj, vt, lam_params, subln_g, *, batch, seq, n_heads, q_col, k_col, tq, tk, lam_init):
    n = batch * seq
    assert seq % tq == 0 and tk % tq == 0 and seq % tk == 0
    nq = seq // tq
    kern = functools.partial(_attn_kernel, tq=tq, tk=tk, lam_init=lam_init)
    return pl.pallas_call(
        kern,
        grid=(batch, n_heads, nq),
        in_specs=[
            pl.BlockSpec(lam_params.shape, lambda b, h, i: (0, 0)),
            pl.BlockSpec((HEAD_WIDTH, 1), lambda b, h, i: (0, 0)),
            pl.BlockSpec((tq, HEAD_WIDTH), lambda b, h, i: (b * nq + i, q_col + h)),
            pl.BlockSpec((seq, HEAD_WIDTH), lambda b, h, i: (b, k_col + h)),
            pl.BlockSpec((HEAD_WIDTH, seq), lambda b, h, i: (h, b)),
        ],
        out_specs=pl.BlockSpec((tq, HEAD_WIDTH), lambda b, h, i: (b * nq + i, h)),
        out_shape=jax.ShapeDtypeStruct((n, n_heads * HEAD_WIDTH), BF16),
        scratch_shapes=[
            pltpu.VMEM((2, 1, tq), F32),
            pltpu.VMEM((2, HEAD_WIDTH + ONES_ROWS, tq), F32),
        ],
        compiler_params=pltpu.CompilerParams(
            dimension_semantics=("arbitrary", "arbitrary", "arbitrary"), vmem_limit_bytes=VMEM_LIMIT),
        name="diff_attention",
    )(lam_params, subln_g, proj, proj, vt)


def _post_kernel(pool_ref, halo_ref, attn_ref, x_ref, wpool_ref, pscale_ref, wout_ref, gffn_ref,
                 wrt_ref, brt_ref, h_ref, xn_ref, meta_ref, *, tm, seq_blocks, n_groups, per_group):
    i = pl.program_id(0)
    blk_in_seq = i % seq_blocks
    halo = halo_ref[...].astype(F32) * jnp.where(blk_in_seq == 0, 0.0, 1.0).astype(F32)
    cur = pool_ref[...].astype(F32)
    ext = jnp.concatenate([halo, cur], axis=0)
    pos = blk_in_seq * tm + lax.broadcasted_iota(jnp.int32, (tm, 1), 0)
    pool_width = cur.shape[1]
    group = pool_width // len(POOL_WINDOWS)

    h = x_ref[...]
    for g, w in enumerate(POOL_WINDOWS):
        sl = slice(g * group, (g + 1) * group)
        s = ext[:, sl]
        span = 1
        while span < w:
            s = s + pltpu.roll(s, span, 0)
            span *= 2
        count = jnp.minimum(pos + 1, w).astype(F32)
        pooled = s[POOL_HALO:, :] / count - cur[:, sl]
        mixed = jnp.dot(pooled.astype(BF16), wpool_ref[g], preferred_element_type=F32) * pscale_ref[:, sl]
        h = h + jnp.dot(mixed.astype(BF16), wout_ref[sl, :], preferred_element_type=F32)
    h = h + jnp.dot(attn_ref[...], wout_ref[pool_width:, :], preferred_element_type=F32)
    h_ref[...] = h

    xn = _rms(h, gffn_ref[...])
    xn_ref[...] = xn
    logits = jnp.dot(xn.astype(BF16), wrt_ref[...], preferred_element_type=F32) + brt_ref[...]

    lane = lax.broadcasted_iota(jnp.int32, logits.shape, 1).astype(F32)
    big = float(ROUTER_LANES)
    is_grp = lane < n_groups
    gl = jnp.where(is_grp, logits, MASK_VALUE)
    gmax = jnp.max(gl, axis=1, keepdims=True)
    gsum = jnp.sum(jnp.where(is_grp, jnp.exp(gl - gmax), 0.0), axis=1, keepdims=True)
    grp_gate = 1.0 / gsum
    gidx = jnp.min(jnp.where(gl == gmax, lane, big), axis=1, keepdims=True)
    lo = n_groups + gidx * per_group
    in_sel = (lane >= lo) & (lane < lo + per_group)
    el = jnp.where(in_sel, logits, MASK_VALUE)
    e1 = jnp.max(el, axis=1, keepdims=True)
    i1 = jnp.min(jnp.where(el == e1, lane, big), axis=1, keepdims=True)
    el2 = jnp.where(lane == i1, MASK_VALUE, el)
    e2 = jnp.max(el2, axis=1, keepdims=True)
    i2 = jnp.min(jnp.where(el2 == e2, lane, big), axis=1, keepdims=True)
    t = jnp.exp(e2 - e1)
    g1 = grp_gate / (1.0 + t)
    g2 = g1 * t
    meta = jnp.where(lane == 0, i1 - n_groups,
                     jnp.where(lane == 1, i2 - n_groups,
                               jnp.where(lane == 2, g1, jnp.where(lane == 3, g2, 0.0))))
    meta_ref[...] = meta


def _post(proj, attn, x2, w_pool, pool_scale, w_out, g_ffn, w_rt, b_rt, *, seq, pool_width, tm,
          n_groups, per_group):
    n, d = x2.shape
    assert n % tm == 0 and seq % tm == 0 and tm % POOL_HALO == 0
    seq_blocks = seq // tm
    halo_per_blk = tm // POOL_HALO
    kern = functools.partial(_post_kernel, tm=tm, seq_blocks=seq_blocks, n_groups=n_groups, per_group=per_group)
    const2 = lambda i: (0, 0)
    return pl.pallas_call(
        kern,
        grid=(n // tm,),
        in_specs=[
            pl.BlockSpec((tm, pool_width), lambda i: (i, 0)),
            pl.BlockSpec((POOL_HALO, pool_width), lambda i: (jnp.maximum(i * halo_per_blk - 1, 0), 0)),
            pl.BlockSpec((tm, attn.shape[1]), lambda i: (i, 0)),
            pl.BlockSpec((tm, d), lambda i: (i, 0)),
            pl.BlockSpec(w_pool.shape, lambda i: (0, 0, 0)),
            pl.BlockSpec(pool_scale.shape, const2),
            pl.BlockSpec(w_out.shape, const2),
            pl.BlockSpec(g_ffn.shape, const2),
            pl.BlockSpec(w_rt.shape, const2),
            pl.BlockSpec(b_rt.shape, const2),
        ],
        out_specs=[
            pl.BlockSpec((tm, d), lambda i: (i, 0)),
            pl.BlockSpec((tm, d), lambda i: (i, 0)),
            pl.BlockSpec((tm, ROUTER_LANES), lambda i: (i, 0)),
        ],
        out_shape=[
            jax.ShapeDtypeStruct((n, d), F32),
            jax.ShapeDtypeStruct((n, d), F32),
            jax.ShapeDtypeStruct((n, ROUTER_LANES), F32),
        ],
        compiler_params=pltpu.CompilerParams(
            dimension_semantics=("arbitrary",), vmem_limit_bytes=VMEM_LIMIT),
        name="post_mix_router",
    )(proj, proj, attn, x2, w_pool, pool_scale, w_out, g_ffn, w_rt, b_rt)


def _moe_kernel(be_ref, nused_ref, rtok_ref, xn_hbm, wg_ref, wu_ref, wd_ref, y_ref,
                xbuf, wg_bf, wu_bf, wd_bf, sem, *, tb):
    b = pl.program_id(0)
    nused = nused_ref[0]

    def row_copy(blk, r, slot):
        tok = rtok_ref[blk * tb + r]
        return pltpu.make_async_copy(xn_hbm.at[pl.ds(tok, 1)], xbuf.at[slot, pl.ds(r, 1)], sem.at[slot])

    def issue(blk, slot):
        def body(r, carry):
            row_copy(blk, r, slot).start()
            return carry
        lax.fori_loop(0, tb, body, 0, unroll=ISSUE_UNROLL)

    @pl.when(b == 0)
    def _():
        issue(0, 0)

    @pl.when(b + 1 < nused)
    def _():
        issue(b + 1, (b + 1) % 2)

    @pl.when(b >= nused)
    def _():
        y_ref[...] = jnp.zeros(y_ref.shape, F32)

    @pl.when(b < nused)
    def _():
        slot = b % 2
        pltpu.make_async_copy(xn_hbm.at[pl.ds(0, tb)], xbuf.at[slot], sem.at[slot]).wait()

        new_expert = (b == 0) | (be_ref[b] != be_ref[jnp.maximum(b - 1, 0)])

        @pl.when(new_expert)
        def _():
            wg_bf[...] = wg_ref[0].astype(BF16)
            wu_bf[...] = wu_ref[0].astype(BF16)
            wd_bf[...] = wd_ref[0].astype(BF16)

        xb = xbuf[slot].astype(BF16)
        hg = jnp.dot(xb, wg_bf[...], preferred_element_type=F32)
        hu = jnp.dot(xb, wu_bf[...], preferred_element_type=F32)
        act = hg * jax.nn.sigmoid(hg) * hu
        y_ref[...] = jnp.dot(act.astype(BF16), wd_bf[...], preferred_element_type=F32)


def _moe(blk_expert, n_used, row_tok, xn, w_gate, w_up, w_down, *, tb):
    n, d = xn.shape
    n_blocks = blk_expert.shape[0]
    de = w_gate.shape[2]
    kern = functools.partial(_moe_kernel, tb=tb)
    grid_spec = pltpu.PrefetchScalarGridSpec(
        num_scalar_prefetch=3,
        grid=(n_blocks,),
        in_specs=[
            pl.BlockSpec(memory_space=pl.ANY),
            pl.BlockSpec((1, d, de), lambda b, be, nu, rt: (be[b], 0, 0)),
            pl.BlockSpec((1, d, de), lambda b, be, nu, rt: (be[b], 0, 0)),
            pl.BlockSpec((1, de, d), lambda b, be, nu, rt: (be[b], 0, 0)),
        ],
        out_specs=pl.BlockSpec((tb, d), lambda b, be, nu, rt: (b, 0)),
        scratch_shapes=[
            pltpu.VMEM((2, tb, d), F32),
            pltpu.VMEM((d, de), BF16),
            pltpu.VMEM((d, de), BF16),
            pltpu.VMEM((de, d), BF16),
            pltpu.SemaphoreType.DMA((2,)),
        ],
    )
    return pl.pallas_call(
        kern,
        grid_spec=grid_spec,
        out_shape=jax.ShapeDtypeStruct((n_blocks * tb, d), F32),
        compiler_params=pltpu.CompilerParams(
            dimension_semantics=("arbitrary",), vmem_limit_bytes=VMEM_LIMIT),
        name="moe_experts",
    )(blk_expert, n_used, row_tok, xn, w_gate, w_up, w_down)


def _combine_kernel(pos_ref, h_ref, meta_ref, g_ref, ys_hbm, o_ref, ybuf, sem, *, tm, final_norm):
    i = pl.program_id(0)
    nsteps = pl.num_programs(0)

    def row_copy(blk, r, k, slot):
        p = pos_ref[(blk * tm + r) * TOP_K_INNER + k]
        return pltpu.make_async_copy(ys_hbm.at[pl.ds(p, 1)], ybuf.at[slot, k, pl.ds(r, 1)], sem.at[slot])

    def issue(blk, slot):
        def body(r, carry):
            for k in range(TOP_K_INNER):
                row_copy(blk, r, k, slot).start()
            return carry
        lax.fori_loop(0, tm, body, 0, unroll=ISSUE_UNROLL)

    @pl.when(i == 0)
    def _():
        issue(0, 0)

    @pl.when(i + 1 < nsteps)
    def _():
        issue(i + 1, (i + 1) % 2)

    slot = i % 2
    for k in range(TOP_K_INNER):
        pltpu.make_async_copy(ys_hbm.at[pl.ds(0, tm)], ybuf.at[slot, k], sem.at[slot]).wait()

    meta = meta_ref[...]
    h = h_ref[...]
    for k in range(TOP_K_INNER):
        h = h + meta[:, TOP_K_INNER + k:TOP_K_INNER + k + 1] * ybuf[slot, k]
    o_ref[...] = _rms(h, g_ref[...]) if final_norm else h


def _combine(pos, h, meta, g_final, ys, *, tm, final_norm):
    n, d = h.shape
    assert n % tm == 0
    kern = functools.partial(_combine_kernel, tm=tm, final_norm=final_norm)
    grid_spec = pltpu.PrefetchScalarGridSpec(
        num_scalar_prefetch=1,
        grid=(n // tm,),
        in_specs=[
            pl.BlockSpec((tm, d), lambda i, p: (i, 0)),
            pl.BlockSpec((tm, ROUTER_LANES), lambda i, p: (i, 0)),
            pl.BlockSpec((1, d), lambda i, p: (0, 0)),
            pl.BlockSpec(memory_space=pl.ANY),
        ],
        out_specs=pl.BlockSpec((tm, d), lambda i, p: (i, 0)),
        scratch_shapes=[
            pltpu.VMEM((2, TOP_K_INNER, tm, d), F32),
            pltpu.SemaphoreType.DMA((2,)),
        ],
    )
    return pl.pallas_call(
        kern,
        grid_spec=grid_spec,
        out_shape=jax.ShapeDtypeStruct((n, d), F32),
        compiler_params=pltpu.CompilerParams(
            dimension_semantics=("arbitrary",), vmem_limit_bytes=VMEM_LIMIT),
        name="combine",
    )(pos, h, meta, g_final, ys)


def _dispatch_plan(expert_id, n_experts, tb):
    n = expert_id.shape[0]
    a = n * TOP_K_INNER
    assert a % RANK_BLOCK == 0
    flat_e = expert_id.reshape(a)
    onehot = flat_e[:, None] == jnp.arange(n_experts, dtype=jnp.int32)[None, :]
    oh = onehot.astype(BF16).reshape(a // RANK_BLOCK, RANK_BLOCK, n_experts)
    tril = jnp.tril(jnp.ones((RANK_BLOCK, RANK_BLOCK), BF16))
    within = jnp.einsum("ij,bjk->bik", tril, oh, preferred_element_type=F32)
    blk_tot = within[:, -1, :]
    blk_off = jnp.cumsum(blk_tot, axis=0) - blk_tot
    incl = (within + blk_off[:, None, :]).reshape(a, n_experts)
    rank = jnp.sum(jnp.where(onehot, incl, 0.0), axis=1).astype(jnp.int32) - 1
    counts = (blk_off[-1] + blk_tot[-1]).astype(jnp.int32)
    padded = (counts + tb - 1) // tb * tb
    pends = jnp.cumsum(padded)
    pstarts = pends - padded
    pos = jnp.sum(jnp.where(onehot, pstarts[None, :], 0), axis=1).astype(jnp.int32) + rank
    n_blocks = -(-(a + n_experts * (tb - 1)) // tb)
    tok = jnp.arange(a, dtype=jnp.int32) // TOP_K_INNER
    row_tok = jnp.zeros((n_blocks * tb,), jnp.int32).at[pos].set(tok, unique_indices=True)
    blk_start = jnp.arange(n_blocks, dtype=jnp.int32) * tb
    blk_expert = jnp.sum((pends[None, :] <= blk_start[:, None]).astype(jnp.int32), axis=1)
    blk_expert = jnp.minimum(blk_expert, n_experts - 1).astype(jnp.int32)
    n_used = (pends[-1] // tb).astype(jnp.int32).reshape(1)
    return blk_expert, n_used, row_tok, pos


def _pick(limit, total):
    t = min(limit, total)
    assert total % t == 0
    return t


def kernel(x, norm_mix_g, w_in, w_pool, pool_scale, lambda_q1, lambda_k1, lambda_q2, lambda_k2, subln_g, w_out, norm_ffn_g, w_grp, b_grp, w_exp, b_exp, w_gate, w_up, w_down, norm_final_g):
    batch, seq, d = x.shape
    n = batch * seq
    depth = w_in.shape[0]
    pool_width = w_pool.shape[1] * w_pool.shape[2]
    attn_width = (w_in.shape[2] - pool_width) // 3
    n_heads = attn_width // HEAD_WIDTH
    n_groups, per_group = w_exp.shape[1], w_exp.shape[3]
    n_experts = n_groups * per_group
    assert n_groups + n_experts <= ROUTER_LANES

    tm_proj = _pick(512, seq)
    tn_proj = _pick(1024, math.gcd(pool_width, attn_width))
    tq = _pick(512, seq)
    tm_post = _pick(256, seq)
    tb = 256
    tm_comb = _pick(256, n)

    cs, sn = _rope_tables(seq)
    h = x.reshape(n, d)
    for l in range(depth):
        lam_init = 0.8 - 0.6 * math.exp(-0.3 * l)
        row_width = pool_width + 2 * attn_width
        w_rows = w_in[l][:, :row_width].astype(BF16)
        w_vt = jnp.transpose(w_in[l][:, row_width:]).astype(BF16)
        proj, vt = _inproj(h, norm_mix_g[l][None, :], w_rows, w_vt, cs, sn, seq=seq,
                           pool_width=pool_width, attn_width=attn_width, tm=tm_proj, tn=tn_proj)
        lam_params = jnp.stack([lambda_q1[l], lambda_k1[l], lambda_q2[l], lambda_k2[l]]).astype(F32)
        col = lambda width: width // HEAD_WIDTH
        attn = _attention(proj, vt, lam_params, subln_g[l][:, None].astype(F32), batch=batch, seq=seq,
                          n_heads=n_heads, q_col=col(pool_width), k_col=col(pool_width + attn_width),
                          tq=tq, tk=min(2 * tq, seq), lam_init=lam_init)

        w_rt = jnp.concatenate([w_grp[l], jnp.transpose(w_exp[l], (1, 0, 2)).reshape(d, n_experts)], axis=1)
        w_rt = jnp.pad(w_rt, ((0, 0), (0, ROUTER_LANES - w_rt.shape[1]))).astype(BF16)
        b_rt = jnp.concatenate([b_grp[l], b_exp[l].reshape(n_experts)]).astype(F32)
        b_rt = jnp.pad(b_rt, (0, ROUTER_LANES - b_rt.shape[0]))[None, :]
        h1, xn, meta = _post(proj, attn, h, w_pool[l].astype(BF16), pool_scale[l][None, :].astype(F32),
                             w_out[l].astype(BF16), norm_ffn_g[l][None, :].astype(F32), w_rt, b_rt,
                             seq=seq, pool_width=pool_width, tm=tm_post, n_groups=n_groups,
                             per_group=per_group)

        expert_id = meta[:, :TOP_K_INNER].astype(jnp.int32)
        blk_expert, n_used, row_tok, pos = _dispatch_plan(expert_id, n_experts, tb)
        ys = _moe(blk_expert, n_used, row_tok, xn, w_gate[l], w_up[l], w_down[l], tb=tb)
        h = _combine(pos, h1, meta, norm_final_g[None, :].astype(F32), ys, tm=tm_comb,
                     final_norm=(l == depth - 1))
    return h.reshape(batch, seq, d)
```

```python
import functools
import math

import jax
import jax.numpy as jnp
from jax import lax
from jax.experimental import pallas as pl
from jax.experimental.pallas import tpu as pltpu

F32 = jnp.float32
BF16 = jnp.bfloat16

EPS = 1e-6
POOL_WINDOWS = (2, 4, 8, 16)
POOL_HALO = 16
HEAD_DIM = 64
HEAD_WIDTH = 2 * HEAD_DIM
ROPE_DIM = HEAD_DIM // 4
ROPE_THETA = 500000.0
Q_SCALE = HEAD_DIM ** -0.5 * math.log2(math.e)
TOP_K_INNER = 2
MASK_VALUE = -1e30
LANES = 128
ROUTER_LANES = 128
ONES_ROWS = 16
ISSUE_UNROLL = 8
RANK_BLOCK = 256
VMEM_LIMIT = 56 * 1024 * 1024


def _rms(x, g):
    ms = jnp.mean(x * x, axis=-1, keepdims=True)
    return x * lax.rsqrt(ms + EPS) * g


def _inproj_kernel(x_ref, g_ref, w_ref, wvt_ref, cs_ref, sn_ref, o_ref, vt_ref, u_ref, *,
                   first_rope_blk, n_rope_blk, n_row_blk):
    j = pl.program_id(1)

    @pl.when(j == 0)
    def _():
        u_ref[...] = _rms(x_ref[...], g_ref[...]).astype(BF16)

    is_rope = (j >= first_rope_blk) & (j < first_rope_blk + 2 * n_rope_blk)

    @pl.when(j < n_row_blk)
    def _():
        acc = jnp.dot(u_ref[...], w_ref[...], preferred_element_type=F32)

        @pl.when(is_rope)
        def _():
            scale = jnp.where(j < first_rope_blk + n_rope_blk, Q_SCALE, 1.0).astype(F32)
            cs = cs_ref[...]
            sn = sn_ref[...]
            lane = lax.broadcasted_iota(jnp.int32, cs.shape, 1)
            first_half = (lane % HEAD_DIM) < (ROPE_DIM // 2)
            for c in range(acc.shape[1] // LANES):
                t = acc[:, c * LANES:(c + 1) * LANES]
                partner = jnp.where(first_half,
                                    pltpu.roll(t, LANES - ROPE_DIM // 2, 1),
                                    pltpu.roll(t, ROPE_DIM // 2, 1))
                o_ref[:, c * LANES:(c + 1) * LANES] = ((t * cs + partner * sn) * scale).astype(BF16)

        @pl.when(jnp.logical_not(is_rope))
        def _():
            o_ref[...] = acc.astype(BF16)

    @pl.when(j == n_row_blk)
    def _():
        vt = lax.dot_general(wvt_ref[...], u_ref[...], (((1,), (1,)), ((), ())), preferred_element_type=F32)
        vt_ref[...] = vt.astype(BF16)


def _inproj(x2, g, w_rows, w_vt, cs, sn, *, seq, pool_width, attn_width, tm, tn):
    n, d = x2.shape
    width = w_rows.shape[1]
    assert n % tm == 0 and seq % tm == 0 and pool_width % tn == 0 and attn_width % tn == 0
    n_row_blk = width // tn
    kern = functools.partial(_inproj_kernel, first_rope_blk=pool_width // tn, n_rope_blk=attn_width // tn,
                             n_row_blk=n_row_blk)
    seq_blocks = seq // tm
    last = n_row_blk - 1
    return pl.pallas_call(
        kern,
        grid=(n // tm, n_row_blk + 1),
        in_specs=[
            pl.BlockSpec((tm, d), lambda i, j: (i, 0)),
            pl.BlockSpec((1, d), lambda i, j: (0, 0)),
            pl.BlockSpec((d, tn), lambda i, j: (0, jnp.minimum(j, last))),
            pl.BlockSpec(w_vt.shape, lambda i, j: (0, 0)),
            pl.BlockSpec((tm, LANES), lambda i, j: (i % seq_blocks, 0)),
            pl.BlockSpec((tm, LANES), lambda i, j: (i % seq_blocks, 0)),
        ],
        out_specs=[
            pl.BlockSpec((tm, tn), lambda i, j: (i, jnp.minimum(j, last))),
            pl.BlockSpec((attn_width, tm), lambda i, j: (0, i)),
        ],
        out_shape=[
            jax.ShapeDtypeStruct((n, width), BF16),
            jax.ShapeDtypeStruct((attn_width, n), BF16),
        ],
        scratch_shapes=[pltpu.VMEM((tm, d), BF16)],
        compiler_params=pltpu.CompilerParams(
            dimension_semantics=("arbitrary", "arbitrary"), vmem_limit_bytes=VMEM_LIMIT),
        name="inproj",
    )(x2, g, w_rows, w_vt, cs, sn)


def _rope_tables(seq):
    half = ROPE_DIM // 2
    inv = ROPE_THETA ** (-jnp.arange(0, ROPE_DIM, 2, dtype=F32) / ROPE_DIM)
    ang = jnp.arange(seq, dtype=F32)[:, None] * inv[None, :]
    cos, sin = jnp.cos(ang), jnp.sin(ang)
    rest = HEAD_DIM - ROPE_DIM
    cs = jnp.concatenate([cos, cos, jnp.ones((seq, rest), F32)], axis=1)
    sn = jnp.concatenate([-sin, sin, jnp.zeros((seq, rest), F32)], axis=1)
    reps = LANES // HEAD_DIM
    return jnp.tile(cs, (1, reps)), jnp.tile(sn, (1, reps))


def _attn_kernel(lam_ref, g_ref, q_ref, k_ref, vt_ref, o_ref, m_ref, acc_ref,
                 s_a, s_b, mx_a, mx_b, p_a, p_b, al_a, al_b, *, tq, tk, lam_init):
    qi = pl.program_id(2)
    q = q_ref[...]
    lane = lax.broadcasted_iota(jnp.int32, q.shape, 1)
    zero = jnp.zeros_like(q)
    qs = (jnp.where(lane < HEAD_DIM, q, zero), jnp.where(lane >= HEAD_DIM, q, zero))

    m_ref[...] = jnp.full(m_ref.shape, MASK_VALUE, F32)
    acc_ref[...] = jnp.zeros(acc_ref.shape, F32)
    p_b[...] = jnp.zeros(p_b.shape, BF16)
    al_b[...] = jnp.ones(al_b.shape, F32)

    def scores(j, s_ref, mx_ref):
        k = k_ref[pl.ds(pl.multiple_of(j * tk, tk), tk), :]
        for c in range(2):
            s = lax.dot_general(k, qs[c], (((1,), (1,)), ((), ())), preferred_element_type=F32)
            s_ref[c] = s
            mx_ref[c] = jnp.max(s, axis=0, keepdims=True)

    def softmax(s_ref, mx_ref, p_ref, al_ref, masked):
        if masked:
            key = qi * tq + lax.broadcasted_iota(jnp.int32, (tk, tq), 0)
            qry = qi * tq + lax.broadcasted_iota(jnp.int32, (tk, tq), 1)
            keep = key <= qry
        for c in range(2):
            s = s_ref[c]
            if masked:
                s = jnp.where(keep, s, MASK_VALUE)
                mx = jnp.max(s, axis=0, keepdims=True)
            else:
                mx = mx_ref[c]
            m_prev = m_ref[c]
            m_new = jnp.maximum(m_prev, mx)
            al_ref[c] = jnp.exp2(m_prev - m_new)
            p_ref[c] = jnp.exp2(s - m_new).astype(BF16)
            m_ref[c] = m_new

    def accumulate(j, p_ref, al_ref):
        start = pl.multiple_of(jnp.maximum(j, 0) * tk, tk)
        vt = jnp.concatenate([vt_ref[:, pl.ds(start, tk)], jnp.ones((ONES_ROWS, tk), BF16)], axis=0)
        for c in range(2):
            acc_ref[c] = acc_ref[c] * al_ref[c] + jnp.dot(vt, p_ref[c], preferred_element_type=F32)

    def step(j, cur, nxt):
        s_cur, mx_cur, p_cur, al_cur = cur
        s_nxt, mx_nxt, p_nxt, al_nxt = nxt
        scores(j + 1, s_nxt, mx_nxt)
        softmax(s_cur, mx_cur, p_cur, al_cur, False)
        accumulate(j - 1, p_nxt, al_nxt)

    def finish(j, cur, nxt):
        s_cur, mx_cur, p_cur, al_cur = cur
        _, _, p_nxt, al_nxt = nxt
        softmax(s_cur, mx_cur, p_cur, al_cur, True)
        accumulate(j - 1, p_nxt, al_nxt)
        accumulate(j, p_cur, al_cur)

    buf_a = (s_a, mx_a, p_a, al_a)
    buf_b = (s_b, mx_b, p_b, al_b)
    scores(0, s_a, mx_a)

    def pair(jj, carry):
        step(2 * jj, buf_a, buf_b)
        step(2 * jj + 1, buf_b, buf_a)
        return carry

    lax.fori_loop(0, qi // 2, pair, 0)

    @pl.when(qi % 2 == 1)
    def _():
        step(qi - 1, buf_a, buf_b)
        finish(qi, buf_b, buf_a)

    @pl.when(qi % 2 == 0)
    def _():
        finish(qi, buf_a, buf_b)

    lp = lam_ref[...]
    lam = (jnp.exp(jnp.sum(lp[0:1] * lp[1:2], axis=1, keepdims=True))
           - jnp.exp(jnp.sum(lp[2:3] * lp[3:4], axis=1, keepdims=True)) + lam_init)
    num = [acc_ref[c][:HEAD_WIDTH, :] for c in range(2)]
    den = [acc_ref[c][HEAD_WIDTH:HEAD_WIDTH + 1, :] for c in range(2)]
    ot = num[0] / den[0] - lam * (num[1] / den[1])
    ms = jnp.mean(ot * ot, axis=0, keepdims=True)
    ot = ot * lax.rsqrt(ms + EPS) * (g_ref[...] * (1.0 - lam_init))
    o_ref[...] = ot.T.astype(BF16)


def _attention(proj, vt, lam_params, subln_g, *, batch, seq, n_heads, q_col, k_col, tq, tk, lam_init):
    n = batch * seq
    assert seq % tq == 0 and tk == tq
    nq = seq // tq
    stage = lambda shape, dtype: [pltpu.VMEM((2,) + shape, dtype)] * 2
    kern = functools.partial(_attn_kernel, tq=tq, tk=tk, lam_init=lam_init)
    return pl.pallas_call(
        kern,
        grid=(batch, n_heads, nq),
        in_specs=[
            pl.BlockSpec(lam_params.shape, lambda b, h, i: (0, 0)),
            pl.BlockSpec((HEAD_WIDTH, 1), lambda b, h, i: (0, 0)),
            pl.BlockSpec((tq, HEAD_WIDTH), lambda b, h, i: (b * nq + i, q_col + h)),
            pl.BlockSpec((seq, HEAD_WIDTH), lambda b, h, i: (b, k_col + h)),
            pl.BlockSpec((HEAD_WIDTH, seq), lambda b, h, i: (h, b)),
        ],
        out_specs=pl.BlockSpec((tq, HEAD_WIDTH), lambda b, h, i: (b * nq + i, h)),
        out_shape=jax.ShapeDtypeStruct((n, n_heads * HEAD_WIDTH), BF16),
        scratch_shapes=[
            pltpu.VMEM((2, 1, tq), F32),
            pltpu.VMEM((2, HEAD_WIDTH + ONES_ROWS, tq), F32),
            *stage((tk, tq), F32), *stage((1, tq), F32), *stage((tk, tq), BF16), *stage((1, tq), F32),
        ],
        compiler_params=pltpu.CompilerParams(
            dimension_semantics=("arbitrary", "arbitrary", "arbitrary"), vmem_limit_bytes=VMEM_LIMIT),
        name="diff_attention",
    )(lam_params, subln_g, proj, proj, vt)


def _post_kernel(pool_ref, halo_ref, attn_ref, x_ref, wpool_ref, pscale_ref, wout_ref, gffn_ref,
                 wrt_ref, brt_ref, h_ref, xn_ref, meta_ref, *, tm, seq_blocks, n_groups, per_group):
    i = pl.program_id(0)
    blk_in_seq = i % seq_blocks
    halo = halo_ref[...].astype(F32) * jnp.where(blk_in_seq == 0, 0.0, 1.0).astype(F32)
    cur = pool_ref[...].astype(F32)
    ext = jnp.concatenate([halo, cur], axis=0)
    pos = blk_in_seq * tm + lax.broadcasted_iota(jnp.int32, (tm, 1), 0)
    pool_width = cur.shape[1]
    group = pool_width // len(POOL_WINDOWS)

    h = x_ref[...]
    for g, w in enumerate(POOL_WINDOWS):
        sl = slice(g * group, (g + 1) * group)
        s = ext[:, sl]
        span = 1
        while span < w:
            s = s + pltpu.roll(s, span, 0)
            span *= 2
        count = jnp.minimum(pos + 1, w).astype(F32)
        pooled = s[POOL_HALO:, :] / count - cur[:, sl]
        mixed = jnp.dot(pooled.astype(BF16), wpool_ref[g], preferred_element_type=F32) * pscale_ref[:, sl]
        h = h + jnp.dot(mixed.astype(BF16), wout_ref[sl, :], preferred_element_type=F32)
    h = h + jnp.dot(attn_ref[...], wout_ref[pool_width:, :], preferred_element_type=F32)
    h_ref[...] = h

    xn = _rms(h, gffn_ref[...])
    xn_ref[...] = xn
    logits = jnp.dot(xn.astype(BF16), wrt_ref[...], preferred_element_type=F32) + brt_ref[...]

    lane = lax.broadcasted_iota(jnp.int32, logits.shape, 1).astype(F32)
    big = float(ROUTER_LANES)
    is_grp = lane < n_groups
    gl = jnp.where(is_grp, logits, MASK_VALUE)
    gmax = jnp.max(gl, axis=1, keepdims=True)
    gsum = jnp.sum(jnp.where(is_grp, jnp.exp(gl - gmax), 0.0), axis=1, keepdims=True)
    grp_gate = 1.0 / gsum
    gidx = jnp.min(jnp.where(gl == gmax, lane, big), axis=1, keepdims=True)
    lo = n_groups + gidx * per_group
    in_sel = (lane >= lo) & (lane < lo + per_group)
    el = jnp.where(in_sel, logits, MASK_VALUE)
    e1 = jnp.max(el, axis=1, keepdims=True)
    i1 = jnp.min(jnp.where(el == e1, lane, big), axis=1, keepdims=True)
    el2 = jnp.where(lane == i1, MASK_VALUE, el)
    e2 = jnp.max(el2, axis=1, keepdims=True)
    i2 = jnp.min(jnp.where(el2 == e2, lane, big), axis=1, keepdims=True)
    t = jnp.exp(e2 - e1)
    g1 = grp_gate / (1.0 + t)
    g2 = g1 * t
    meta = jnp.where(lane == 0, i1 - n_groups,
                     jnp.where(lane == 1, i2 - n_groups,
                               jnp.where(lane == 2, g1, jnp.where(lane == 3, g2, 0.0))))
    meta_ref[...] = meta


def _post(proj, attn, x2, w_pool, pool_scale, w_out, g_ffn, w_rt, b_rt, *, seq, pool_width, tm,
          n_groups, per_group):
    n, d = x2.shape
    assert n % tm == 0 and seq % tm == 0 and tm % POOL_HALO == 0
    seq_blocks = seq // tm
    halo_per_blk = tm // POOL_HALO
    kern = functools.partial(_post_kernel, tm=tm, seq_blocks=seq_blocks, n_groups=n_groups, per_group=per_group)
    const2 = lambda i: (0, 0)
    return pl.pallas_call(
        kern,
        grid=(n // tm,),
        in_specs=[
            pl.BlockSpec((tm, pool_width), lambda i: (i, 0)),
            pl.BlockSpec((POOL_HALO, pool_width), lambda i: (jnp.maximum(i * halo_per_blk - 1, 0), 0)),
            pl.BlockSpec((tm, attn.shape[1]), lambda i: (i, 0)),
            pl.BlockSpec((tm, d), lambda i: (i, 0)),
            pl.BlockSpec(w_pool.shape, lambda i: (0, 0, 0)),
            pl.BlockSpec(pool_scale.shape, const2),
            pl.BlockSpec(w_out.shape, const2),
            pl.BlockSpec(g_ffn.shape, const2),
            pl.BlockSpec(w_rt.shape, const2),
            pl.BlockSpec(b_rt.shape, const2),
        ],
        out_specs=[
            pl.BlockSpec((tm, d), lambda i: (i, 0)),
            pl.BlockSpec((tm, d), lambda i: (i, 0)),
            pl.BlockSpec((tm, ROUTER_LANES), lambda i: (i, 0)),
        ],
        out_shape=[
            jax.ShapeDtypeStruct((n, d), F32),
            jax.ShapeDtypeStruct((n, d), F32),
            jax.ShapeDtypeStruct((n, ROUTER_LANES), F32),
        ],
        compiler_params=pltpu.CompilerParams(
            dimension_semantics=("arbitrary",), vmem_limit_bytes=VMEM_LIMIT),
        name="post_mix_router",
    )(proj, proj, attn, x2, w_pool, pool_scale, w_out, g_ffn, w_rt, b_rt)


def _moe_kernel(be_ref, nused_ref, rtok_ref, xn_hbm, wg_ref, wu_ref, wd_ref, y_ref,
                xbuf, wg_bf, wu_bf, wd_bf, sem, *, tb):
    b = pl.program_id(0)
    nused = nused_ref[0]

    def row_copy(blk, r, slot):
        tok = rtok_ref[blk * tb + r]
        return pltpu.make_async_copy(xn_hbm.at[pl.ds(tok, 1)], xbuf.at[slot, pl.ds(r, 1)], sem.at[slot])

    def issue(blk, slot):
        def body(r, carry):
            row_copy(blk, r, slot).start()
            return carry
        lax.fori_loop(0, tb, body, 0, unroll=ISSUE_UNROLL)

    @pl.when(b == 0)
    def _():
        issue(0, 0)

    @pl.when(b + 1 < nused)
    def _():
        issue(b + 1, (b + 1) % 2)

    @pl.when(b >= nused)
    def _():
        y_ref[...] = jnp.zeros(y_ref.shape, F32)

    @pl.when(b < nused)
    def _():
        slot = b % 2
        pltpu.make_async_copy(xn_hbm.at[pl.ds(0, tb)], xbuf.at[slot], sem.at[slot]).wait()

        new_expert = (b == 0) | (be_ref[b] != be_ref[jnp.maximum(b - 1, 0)])

        @pl.when(new_expert)
        def _():
            wg_bf[...] = wg_ref[0].astype(BF16)
            wu_bf[...] = wu_ref[0].astype(BF16)
            wd_bf[...] = wd_ref[0].astype(BF16)

        xb = xbuf[slot].astype(BF16)
        hg = jnp.dot(xb, wg_bf[...], preferred_element_type=F32)
        hu = jnp.dot(xb, wu_bf[...], preferred_element_type=F32)
        act = hg * jax.nn.sigmoid(hg) * hu
        y_ref[...] = jnp.dot(act.astype(BF16), wd_bf[...], preferred_element_type=F32)


def _moe(blk_expert, n_used, row_tok, xn, w_gate, w_up, w_down, *, tb):
    n, d = xn.shape
    n_blocks = blk_expert.shape[0]
    de = w_gate.shape[2]
    kern = functools.partial(_moe_kernel, tb=tb)
    grid_spec = pltpu.PrefetchScalarGridSpec(
        num_scalar_prefetch=3,
        grid=(n_blocks,),
        in_specs=[
            pl.BlockSpec(memory_space=pl.ANY),
            pl.BlockSpec((1, d, de), lambda b, be, nu, rt: (be[b], 0, 0)),
            pl.BlockSpec((1, d, de), lambda b, be, nu, rt: (be[b], 0, 0)),
            pl.BlockSpec((1, de, d), lambda b, be, nu, rt: (be[b], 0, 0)),
        ],
        out_specs=pl.BlockSpec((tb, d), lambda b, be, nu, rt: (b, 0)),
        scratch_shapes=[
            pltpu.VMEM((2, tb, d), F32),
            pltpu.VMEM((d, de), BF16),
            pltpu.VMEM((d, de), BF16),
            pltpu.VMEM((de, d), BF16),
            pltpu.SemaphoreType.DMA((2,)),
        ],
    )
    return pl.pallas_call(
        kern,
        grid_spec=grid_spec,
        out_shape=jax.ShapeDtypeStruct((n_blocks * tb, d), F32),
        compiler_params=pltpu.CompilerParams(
            dimension_semantics=("arbitrary",), vmem_limit_bytes=VMEM_LIMIT),
        name="moe_experts",
    )(blk_expert, n_used, row_tok, xn, w_gate, w_up, w_down)


def _combine_kernel(pos_ref, h_ref, meta_ref, g_ref, ys_hbm, o_ref, ybuf, sem, *, tm, final_norm):
    i = pl.program_id(0)
    nsteps = pl.num_programs(0)

    def row_copy(blk, r, k, slot):
        p = pos_ref[(blk * tm + r) * TOP_K_INNER + k]
        return pltpu.make_async_copy(ys_hbm.at[pl.ds(p, 1)], ybuf.at[slot, k, pl.ds(r, 1)], sem.at[slot])

    def issue(blk, slot):
        def body(r, carry):
            for k in range(TOP_K_INNER):
                row_copy(blk, r, k, slot).start()
            return carry
        lax.fori_loop(0, tm, body, 0, unroll=ISSUE_UNROLL)

    @pl.when(i == 0)
    def _():
        issue(0, 0)

    @pl.when(i + 1 < nsteps)
    def _():
        issue(i + 1, (i + 1) % 2)

    slot = i % 2
    for k in range(TOP_K_INNER):
        pltpu.make_async_copy(ys_hbm.at[pl.ds(0, tm)], ybuf.at[slot, k], sem.at[slot]).wait()

    meta = meta_ref[...]
    h = h_ref[...]
    for k in range(TOP_K_INNER):
        h = h + meta[:, TOP_K_INNER + k:TOP_K_INNER + k + 1] * ybuf[slot, k]
    o_ref[...] = _rms(h, g_ref[...]) if final_norm else h


def _combine(pos, h, meta, g_final, ys, *, tm, final_norm):
    n, d = h.shape
    assert n % tm == 0
    kern = functools.partial(_combine_kernel, tm=tm, final_norm=final_norm)
    grid_spec = pltpu.PrefetchScalarGridSpec(
        num_scalar_prefetch=1,
        grid=(n // tm,),
        in_specs=[
            pl.BlockSpec((tm, d), lambda i, p: (i, 0)),
            pl.BlockSpec((tm, ROUTER_LANES), lambda i, p: (i, 0)),
            pl.BlockSpec((1, d), lambda i, p: (0, 0)),
            pl.BlockSpec(memory_space=pl.ANY),
        ],
        out_specs=pl.BlockSpec((tm, d), lambda i, p: (i, 0)),
        scratch_shapes=[
            pltpu.VMEM((2, TOP_K_INNER, tm, d), F32),
            pltpu.SemaphoreType.DMA((2,)),
        ],
    )
    return pl.pallas_call(
        kern,
        grid_spec=grid_spec,
        out_shape=jax.ShapeDtypeStruct((n, d), F32),
        compiler_params=pltpu.CompilerParams(
            dimension_semantics=("arbitrary",), vmem_limit_bytes=VMEM_LIMIT),
        name="combine",
    )(pos, h, meta, g_final, ys)


def _dispatch_plan(expert_id, n_experts, tb):
    n = expert_id.shape[0]
    a = n * TOP_K_INNER
    assert a % RANK_BLOCK == 0
    flat_e = expert_id.reshape(a)
    onehot = flat_e[:, None] == jnp.arange(n_experts, dtype=jnp.int32)[None, :]
    oh = onehot.astype(BF16).reshape(a // RANK_BLOCK, RANK_BLOCK, n_experts)
    tril = jnp.tril(jnp.ones((RANK_BLOCK, RANK_BLOCK), BF16))
    within = jnp.einsum("ij,bjk->bik", tril, oh, preferred_element_type=F32)
    blk_tot = within[:, -1, :]
    blk_off = jnp.cumsum(blk_tot, axis=0) - blk_tot
    incl = (within + blk_off[:, None, :]).reshape(a, n_experts)
    rank = jnp.sum(jnp.where(onehot, incl, 0.0), axis=1).astype(jnp.int32) - 1
    counts = (blk_off[-1] + blk_tot[-1]).astype(jnp.int32)
    padded = (counts + tb - 1) // tb * tb
    pends = jnp.cumsum(padded)
    pstarts = pends - padded
    pos = jnp.sum(jnp.where(onehot, pstarts[None, :], 0), axis=1).astype(jnp.int32) + rank
    n_blocks = -(-(a + n_experts * (tb - 1)) // tb)
    tok = jnp.arange(a, dtype=jnp.int32) // TOP_K_INNER
    row_tok = jnp.zeros((n_blocks * tb,), jnp.int32).at[pos].set(tok, unique_indices=True)
    blk_start = jnp.arange(n_blocks, dtype=jnp.int32) * tb
    blk_expert = jnp.sum((pends[None, :] <= blk_start[:, None]).astype(jnp.int32), axis=1)
    blk_expert = jnp.minimum(blk_expert, n_experts - 1).astype(jnp.int32)
    n_used = (pends[-1] // tb).astype(jnp.int32).reshape(1)
    return blk_expert, n_used, row_tok, pos


def _pick(limit, total):
    t = min(limit, total)
    assert total % t == 0
    return t


def kernel(x, norm_mix_g, w_in, w_pool, pool_scale, lambda_q1, lambda_k1, lambda_q2, lambda_k2, subln_g, w_out, norm_ffn_g, w_grp, b_grp, w_exp, b_exp, w_gate, w_up, w_down, norm_final_g):
    batch, seq, d = x.shape
    n = batch * seq
    depth = w_in.shape[0]
    pool_width = w_pool.shape[1] * w_pool.shape[2]
    attn_width = (w_in.shape[2] - pool_width) // 3
    n_heads = attn_width // HEAD_WIDTH
    n_groups, per_group = w_exp.shape[1], w_exp.shape[3]
    n_experts = n_groups * per_group
    assert n_groups + n_experts <= ROUTER_LANES

    tm_proj = _pick(512, seq)
    tn_proj = _pick(1024, math.gcd(pool_width, attn_width))
    tq = _pick(512, seq)
    tm_post = _pick(256, seq)
    tb = 256
    tm_comb = _pick(256, n)

    cs, sn = _rope_tables(seq)
    h = x.reshape(n, d)
    for l in range(depth):
        lam_init = 0.8 - 0.6 * math.exp(-0.3 * l)
        row_width = pool_width + 2 * attn_width
        w_rows = w_in[l][:, :row_width].astype(BF16)
        w_vt = jnp.transpose(w_in[l][:, row_width:]).astype(BF16)
        proj, vt = _inproj(h, norm_mix_g[l][None, :], w_rows, w_vt, cs, sn, seq=seq,
                           pool_width=pool_width, attn_width=attn_width, tm=tm_proj, tn=tn_proj)
        lam_params = jnp.stack([lambda_q1[l], lambda_k1[l], lambda_q2[l], lambda_k2[l]]).astype(F32)
        col = lambda width: width // HEAD_WIDTH
        attn = _attention(proj, vt, lam_params, subln_g[l][:, None].astype(F32), batch=batch, seq=seq,
                          n_heads=n_heads, q_col=col(pool_width), k_col=col(pool_width + attn_width),
                          tq=tq, tk=tq, lam_init=lam_init)

        w_rt = jnp.concatenate([w_grp[l], jnp.transpose(w_exp[l], (1, 0, 2)).reshape(d, n_experts)], axis=1)
        w_rt = jnp.pad(w_rt, ((0, 0), (0, ROUTER_LANES - w_rt.shape[1]))).astype(BF16)
        b_rt = jnp.concatenate([b_grp[l], b_exp[l].reshape(n_experts)]).astype(F32)
        b_rt = jnp.pad(b_rt, (0, ROUTER_LANES - b_rt.shape[0]))[None, :]
        h1, xn, meta = _post(proj, attn, h, w_pool[l].astype(BF16), pool_scale[l][None, :].astype(F32),
                             w_out[l].astype(BF16), norm_ffn_g[l][None, :].astype(F32), w_rt, b_rt,
                             seq=seq, pool_width=pool_width, tm=tm_post, n_groups=n_groups,
                             per_group=per_group)

        expert_id = meta[:, :TOP_K_INNER].astype(jnp.int32)
        blk_expert, n_used, row_tok, pos = _dispatch_plan(expert_id, n_experts, tb)
        ys = _moe(blk_expert, n_used, row_tok, xn, w_gate[l], w_up[l], w_down[l], tb=tb)
        h = _combine(pos, h1, meta, norm_final_g[None, :].astype(F32), ys, tm=tm_comb,
                     final_norm=(l == depth - 1))
    return h.reshape(batch, seq, d)
```

```python
import functools
import math

import jax
import jax.numpy as jnp
from jax import lax
from jax.experimental import pallas as pl
from jax.experimental.pallas import tpu as pltpu

F32 = jnp.float32
BF16 = jnp.bfloat16

EPS = 1e-6
POOL_WINDOWS = (2, 4, 8, 16)
POOL_HALO = 16
HEAD_DIM = 64
HEAD_WIDTH = 2 * HEAD_DIM
ROPE_DIM = HEAD_DIM // 4
ROPE_THETA = 500000.0
Q_SCALE = HEAD_DIM ** -0.5 * math.log2(math.e)
TOP_K_INNER = 2
MASK_VALUE = -1e30
LANES = 128
ROUTER_LANES = 128
ONES_ROWS = 16
ISSUE_UNROLL = 8
RANK_BLOCK = 256
VMEM_LIMIT = 56 * 1024 * 1024


def _rms(x, g):
    ms = jnp.mean(x * x, axis=-1, keepdims=True)
    return x * lax.rsqrt(ms + EPS) * g


def _pack_bf16_pairs(x):
    half = x.shape[1] // 2
    rounded = x.astype(BF16).astype(F32)
    hi_bits = lax.bitcast_convert_type(rounded[:, :half], jnp.uint32)
    lo_bits = lax.bitcast_convert_type(rounded[:, half:], jnp.uint32)
    return hi_bits | (lo_bits >> 16)


def _unpack_bf16_pairs(words):
    hi = lax.bitcast_convert_type(words & jnp.uint32(0xFFFF0000), F32)
    lo = lax.bitcast_convert_type(words << 16, F32)
    return hi, lo


def _inproj_kernel(x_ref, g_ref, w_ref, wvt_ref, cs_ref, sn_ref, o_ref, vt_ref, u_ref, *,
                   first_rope_blk, n_rope_blk, n_row_blk):
    j = pl.program_id(1)

    @pl.when(j == 0)
    def _():
        u_ref[...] = _rms(x_ref[...], g_ref[...]).astype(BF16)

    is_rope = (j >= first_rope_blk) & (j < first_rope_blk + 2 * n_rope_blk)

    @pl.when(j < n_row_blk)
    def _():
        acc = jnp.dot(u_ref[...], w_ref[...], preferred_element_type=F32)

        @pl.when(is_rope)
        def _():
            scale = jnp.where(j < first_rope_blk + n_rope_blk, Q_SCALE, 1.0).astype(F32)
            cs = cs_ref[...]
            sn = sn_ref[...]
            lane = lax.broadcasted_iota(jnp.int32, cs.shape, 1)
            first_half = (lane % HEAD_DIM) < (ROPE_DIM // 2)
            for c in range(acc.shape[1] // LANES):
                t = acc[:, c * LANES:(c + 1) * LANES]
                partner = jnp.where(first_half,
                                    pltpu.roll(t, LANES - ROPE_DIM // 2, 1),
                                    pltpu.roll(t, ROPE_DIM // 2, 1))
                o_ref[:, c * LANES:(c + 1) * LANES] = ((t * cs + partner * sn) * scale).astype(BF16)

        @pl.when(jnp.logical_not(is_rope))
        def _():
            o_ref[...] = acc.astype(BF16)

    @pl.when(j == n_row_blk)
    def _():
        vt = lax.dot_general(wvt_ref[...], u_ref[...], (((1,), (1,)), ((), ())), preferred_element_type=F32)
        vt_ref[...] = vt.astype(BF16)


def _inproj(x2, g, w_rows, w_vt, cs, sn, *, seq, pool_width, attn_width, tm, tn):
    n, d = x2.shape
    width = w_rows.shape[1]
    assert n % tm == 0 and seq % tm == 0 and pool_width % tn == 0 and attn_width % tn == 0
    n_row_blk = width // tn
    kern = functools.partial(_inproj_kernel, first_rope_blk=pool_width // tn, n_rope_blk=attn_width // tn,
                             n_row_blk=n_row_blk)
    seq_blocks = seq // tm
    last = n_row_blk - 1
    return pl.pallas_call(
        kern,
        grid=(n // tm, n_row_blk + 1),
        in_specs=[
            pl.BlockSpec((tm, d), lambda i, j: (i, 0)),
            pl.BlockSpec((1, d), lambda i, j: (0, 0)),
            pl.BlockSpec((d, tn), lambda i, j: (0, jnp.minimum(j, last))),
            pl.BlockSpec(w_vt.shape, lambda i, j: (0, 0)),
            pl.BlockSpec((tm, LANES), lambda i, j: (i % seq_blocks, 0)),
            pl.BlockSpec((tm, LANES), lambda i, j: (i % seq_blocks, 0)),
        ],
        out_specs=[
            pl.BlockSpec((tm, tn), lambda i, j: (i, jnp.minimum(j, last))),
            pl.BlockSpec((attn_width, tm), lambda i, j: (0, i)),
        ],
        out_shape=[
            jax.ShapeDtypeStruct((n, width), BF16),
            jax.ShapeDtypeStruct((attn_width, n), BF16),
        ],
        scratch_shapes=[pltpu.VMEM((tm, d), BF16)],
        compiler_params=pltpu.CompilerParams(
            dimension_semantics=("arbitrary", "arbitrary"), vmem_limit_bytes=VMEM_LIMIT),
        name="inproj",
    )(x2, g, w_rows, w_vt, cs, sn)


def _rope_tables(seq):
    half = ROPE_DIM // 2
    inv = ROPE_THETA ** (-jnp.arange(0, ROPE_DIM, 2, dtype=F32) / ROPE_DIM)
    ang = jnp.arange(seq, dtype=F32)[:, None] * inv[None, :]
    cos, sin = jnp.cos(ang), jnp.sin(ang)
    rest = HEAD_DIM - ROPE_DIM
    cs = jnp.concatenate([cos, cos, jnp.ones((seq, rest), F32)], axis=1)
    sn = jnp.concatenate([-sin, sin, jnp.zeros((seq, rest), F32)], axis=1)
    reps = LANES // HEAD_DIM
    return jnp.tile(cs, (1, reps)), jnp.tile(sn, (1, reps))


def _attn_kernel(lam_ref, g_ref, q_ref, k_ref, vt_ref, o_ref, m_ref, acc_ref,
                 s_a, s_b, mx_a, mx_b, p_a, p_b, al_a, al_b, *, tq, tk, lam_init):
    qi = pl.program_id(2)
    q = q_ref[...]
    lane = lax.broadcasted_iota(jnp.int32, q.shape, 1)
    zero = jnp.zeros_like(q)
    qs = (jnp.where(lane < HEAD_DIM, q, zero), jnp.where(lane >= HEAD_DIM, q, zero))

    m_ref[...] = jnp.full(m_ref.shape, MASK_VALUE, F32)
    acc_ref[...] = jnp.zeros(acc_ref.shape, F32)
    p_b[...] = jnp.zeros(p_b.shape, BF16)
    al_b[...] = jnp.ones(al_b.shape, F32)

    def scores(j, s_ref, mx_ref):
        k = k_ref[pl.ds(pl.multiple_of(j * tk, tk), tk), :]
        for c in range(2):
            s = lax.dot_general(k, qs[c], (((1,), (1,)), ((), ())), preferred_element_type=F32)
            s_ref[c] = s
            mx_ref[c] = jnp.max(s, axis=0, keepdims=True)

    def softmax(s_ref, mx_ref, p_ref, al_ref, masked):
        if masked:
            key = qi * tq + lax.broadcasted_iota(jnp.int32, (tk, tq), 0)
            qry = qi * tq + lax.broadcasted_iota(jnp.int32, (tk, tq), 1)
            keep = key <= qry
        for c in range(2):
            s = s_ref[c]
            if masked:
                s = jnp.where(keep, s, MASK_VALUE)
                mx = jnp.max(s, axis=0, keepdims=True)
            else:
                mx = mx_ref[c]
            m_prev = m_ref[c]
            m_new = jnp.maximum(m_prev, mx)
            al_ref[c] = jnp.exp2(m_prev - m_new)
            p_ref[c] = jnp.exp2(s - m_new).astype(BF16)
            m_ref[c] = m_new

    def accumulate(j, p_ref, al_ref):
        start = pl.multiple_of(jnp.maximum(j, 0) * tk, tk)
        vt = jnp.concatenate([vt_ref[:, pl.ds(start, tk)], jnp.ones((ONES_ROWS, tk), BF16)], axis=0)
        for c in range(2):
            acc_ref[c] = acc_ref[c] * al_ref[c] + jnp.dot(vt, p_ref[c], preferred_element_type=F32)

    def step(j, cur, nxt):
        s_cur, mx_cur, p_cur, al_cur = cur
        s_nxt, mx_nxt, p_nxt, al_nxt = nxt
        scores(j + 1, s_nxt, mx_nxt)
        softmax(s_cur, mx_cur, p_cur, al_cur, False)
        accumulate(j - 1, p_nxt, al_nxt)

    def finish(j, cur, nxt):
        s_cur, mx_cur, p_cur, al_cur = cur
        _, _, p_nxt, al_nxt = nxt
        softmax(s_cur, mx_cur, p_cur, al_cur, True)
        accumulate(j - 1, p_nxt, al_nxt)
        accumulate(j, p_cur, al_cur)

    buf_a = (s_a, mx_a, p_a, al_a)
    buf_b = (s_b, mx_b, p_b, al_b)
    scores(0, s_a, mx_a)

    def pair(jj, carry):
        step(2 * jj, buf_a, buf_b)
        step(2 * jj + 1, buf_b, buf_a)
        return carry

    lax.fori_loop(0, qi // 2, pair, 0)

    @pl.when(qi % 2 == 1)
    def _():
        step(qi - 1, buf_a, buf_b)
        finish(qi, buf_b, buf_a)

    @pl.when(qi % 2 == 0)
    def _():
        finish(qi, buf_a, buf_b)

    lp = lam_ref[...]
    lam = (jnp.exp(jnp.sum(lp[0:1] * lp[1:2], axis=1, keepdims=True))
           - jnp.exp(jnp.sum(lp[2:3] * lp[3:4], axis=1, keepdims=True)) + lam_init)
    num = [acc_ref[c][:HEAD_WIDTH, :] for c in range(2)]
    den = [acc_ref[c][HEAD_WIDTH:HEAD_WIDTH + 1, :] for c in range(2)]
    ot = num[0] / den[0] - lam * (num[1] / den[1])
    ms = jnp.mean(ot * ot, axis=0, keepdims=True)
    ot = ot * lax.rsqrt(ms + EPS) * (g_ref[...] * (1.0 - lam_init))
    o_ref[...] = ot.T.astype(BF16)


def _attention(proj, vt, lam_params, subln_g, *, batch, seq, n_heads, q_col, k_col, tq, tk, lam_init):
    n = batch * seq
    assert seq % tq == 0 and tk == tq
    nq = seq // tq
    stage = lambda shape, dtype: [pltpu.VMEM((2,) + shape, dtype)] * 2
    kern = functools.partial(_attn_kernel, tq=tq, tk=tk, lam_init=lam_init)
    return pl.pallas_call(
        kern,
        grid=(batch, n_heads, nq),
        in_specs=[
            pl.BlockSpec(lam_params.shape, lambda b, h, i: (0, 0)),
            pl.BlockSpec((HEAD_WIDTH, 1), lambda b, h, i: (0, 0)),
            pl.BlockSpec((tq, HEAD_WIDTH), lambda b, h, i: (b * nq + i, q_col + h)),
            pl.BlockSpec((seq, HEAD_WIDTH), lambda b, h, i: (b, k_col + h)),
            pl.BlockSpec((HEAD_WIDTH, seq), lambda b, h, i: (h, b)),
        ],
        out_specs=pl.BlockSpec((tq, HEAD_WIDTH), lambda b, h, i: (b * nq + i, h)),
        out_shape=jax.ShapeDtypeStruct((n, n_heads * HEAD_WIDTH), BF16),
        scratch_shapes=[
            pltpu.VMEM((2, 1, tq), F32),
            pltpu.VMEM((2, HEAD_WIDTH + ONES_ROWS, tq), F32),
            *stage((tk, tq), F32), *stage((1, tq), F32), *stage((tk, tq), BF16), *stage((1, tq), F32),
        ],
        compiler_params=pltpu.CompilerParams(
            dimension_semantics=("arbitrary", "arbitrary", "arbitrary"), vmem_limit_bytes=VMEM_LIMIT),
        name="diff_attention",
    )(lam_params, subln_g, proj, proj, vt)


def _post_kernel(pool_ref, halo_ref, attn_ref, x_ref, wpool_ref, pscale_ref, wout_ref, gffn_ref,
                 wrt_ref, brt_ref, h_ref, xn_ref, meta_ref, *, tm, seq_blocks, n_groups, per_group):
    i = pl.program_id(0)
    blk_in_seq = i % seq_blocks
    halo = halo_ref[...].astype(F32) * jnp.where(blk_in_seq == 0, 0.0, 1.0).astype(F32)
    cur = pool_ref[...].astype(F32)
    ext = jnp.concatenate([halo, cur], axis=0)
    pos = blk_in_seq * tm + lax.broadcasted_iota(jnp.int32, (tm, 1), 0)
    pool_width = cur.shape[1]
    group = pool_width // len(POOL_WINDOWS)

    h = x_ref[...]
    for g, w in enumerate(POOL_WINDOWS):
        sl = slice(g * group, (g + 1) * group)
        s = ext[:, sl]
        span = 1
        while span < w:
            s = s + pltpu.roll(s, span, 0)
            span *= 2
        count = jnp.minimum(pos + 1, w).astype(F32)
        pooled = s[POOL_HALO:, :] / count - cur[:, sl]
        mixed = jnp.dot(pooled.astype(BF16), wpool_ref[g], preferred_element_type=F32) * pscale_ref[:, sl]
        h = h + jnp.dot(mixed.astype(BF16), wout_ref[sl, :], preferred_element_type=F32)
    h = h + jnp.dot(attn_ref[...], wout_ref[pool_width:, :], preferred_element_type=F32)
    h_ref[...] = h

    xn = _rms(h, gffn_ref[...])
    xn_ref[...] = _pack_bf16_pairs(xn)
    logits = jnp.dot(xn.astype(BF16), wrt_ref[...], preferred_element_type=F32) + brt_ref[...]

    lane = lax.broadcasted_iota(jnp.int32, logits.shape, 1).astype(F32)
    big = float(ROUTER_LANES)
    is_grp = lane < n_groups
    gl = jnp.where(is_grp, logits, MASK_VALUE)
    gmax = jnp.max(gl, axis=1, keepdims=True)
    gsum = jnp.sum(jnp.where(is_grp, jnp.exp(gl - gmax), 0.0), axis=1, keepdims=True)
    grp_gate = 1.0 / gsum
    gidx = jnp.min(jnp.where(gl == gmax, lane, big), axis=1, keepdims=True)
    lo = n_groups + gidx * per_group
    in_sel = (lane >= lo) & (lane < lo + per_group)
    el = jnp.where(in_sel, logits, MASK_VALUE)
    e1 = jnp.max(el, axis=1, keepdims=True)
    i1 = jnp.min(jnp.where(el == e1, lane, big), axis=1, keepdims=True)
    el2 = jnp.where(lane == i1, MASK_VALUE, el)
    e2 = jnp.max(el2, axis=1, keepdims=True)
    i2 = jnp.min(jnp.where(el2 == e2, lane, big), axis=1, keepdims=True)
    t = jnp.exp(e2 - e1)
    g1 = grp_gate / (1.0 + t)
    g2 = g1 * t
    meta = jnp.where(lane == 0, i1 - n_groups,
                     jnp.where(lane == 1, i2 - n_groups,
                               jnp.where(lane == 2, g1, jnp.where(lane == 3, g2, 0.0))))
    meta_ref[...] = meta


def _post(proj, attn, x2, w_pool, pool_scale, w_out, g_ffn, w_rt, b_rt, *, seq, pool_width, tm,
          n_groups, per_group):
    n, d = x2.shape
    assert n % tm == 0 and seq % tm == 0 and tm % POOL_HALO == 0
    seq_blocks = seq // tm
    halo_per_blk = tm // POOL_HALO
    kern = functools.partial(_post_kernel, tm=tm, seq_blocks=seq_blocks, n_groups=n_groups, per_group=per_group)
    const2 = lambda i: (0, 0)
    return pl.pallas_call(
        kern,
        grid=(n // tm,),
        in_specs=[
            pl.BlockSpec((tm, pool_width), lambda i: (i, 0)),
            pl.BlockSpec((POOL_HALO, pool_width), lambda i: (jnp.maximum(i * halo_per_blk - 1, 0), 0)),
            pl.BlockSpec((tm, attn.shape[1]), lambda i: (i, 0)),
            pl.BlockSpec((tm, d), lambda i: (i, 0)),
            pl.BlockSpec(w_pool.shape, lambda i: (0, 0, 0)),
            pl.BlockSpec(pool_scale.shape, const2),
            pl.BlockSpec(w_out.shape, const2),
            pl.BlockSpec(g_ffn.shape, const2),
            pl.BlockSpec(w_rt.shape, const2),
            pl.BlockSpec(b_rt.shape, const2),
        ],
        out_specs=[
            pl.BlockSpec((tm, d), lambda i: (i, 0)),
            pl.BlockSpec((tm, d // 2), lambda i: (i, 0)),
            pl.BlockSpec((tm, ROUTER_LANES), lambda i: (i, 0)),
        ],
        out_shape=[
            jax.ShapeDtypeStruct((n, d), F32),
            jax.ShapeDtypeStruct((n, d // 2), jnp.uint32),
            jax.ShapeDtypeStruct((n, ROUTER_LANES), F32),
        ],
        compiler_params=pltpu.CompilerParams(
            dimension_semantics=("arbitrary",), vmem_limit_bytes=VMEM_LIMIT),
        name="post_mix_router",
    )(proj, proj, attn, x2, w_pool, pool_scale, w_out, g_ffn, w_rt, b_rt)


def _moe_kernel(be_ref, nused_ref, rtok_ref, xn_hbm, wg_ref, wu_ref, wd_ref, y_ref,
                xbuf, xb_ref, wg_bf, wu_bf, wd_bf, sem, *, tb):
    b = pl.program_id(0)
    nused = nused_ref[0]

    def row_copy(blk, r, slot):
        tok = rtok_ref[blk * tb + r]
        return pltpu.make_async_copy(xn_hbm.at[pl.ds(tok, 1)], xbuf.at[slot, pl.ds(r, 1)], sem.at[slot])

    def issue(blk, slot):
        def body(r, carry):
            row_copy(blk, r, slot).start()
            return carry
        lax.fori_loop(0, tb, body, 0, unroll=ISSUE_UNROLL)

    @pl.when(b == 0)
    def _():
        issue(0, 0)

    @pl.when(b >= nused)
    def _():
        y_ref[...] = jnp.zeros(y_ref.shape, jnp.uint32)

    def expert_block(prefetch_next):
        slot = b % 2
        pltpu.make_async_copy(xn_hbm.at[pl.ds(0, tb)], xbuf.at[slot], sem.at[slot]).wait()
        hi, lo = _unpack_bf16_pairs(xbuf[slot])
        half = hi.shape[1]
        xb_ref[:, :half] = hi.astype(BF16)
        xb_ref[:, half:] = lo.astype(BF16)

        new_expert = (b == 0) | (be_ref[b] != be_ref[jnp.maximum(b - 1, 0)])

        @pl.when(new_expert)
        def _():
            wg_bf[...] = wg_ref[0].astype(BF16)
            wu_bf[...] = wu_ref[0].astype(BF16)
            wd_bf[...] = wd_ref[0].astype(BF16)

        if prefetch_next:
            for r in range(tb):
                row_copy(b + 1, r, 1 - slot).start()
        xb = xb_ref[...]
        hg = jnp.dot(xb, wg_bf[...], preferred_element_type=F32)
        hu = jnp.dot(xb, wu_bf[...], preferred_element_type=F32)
        act = hg * jax.nn.sigmoid(hg) * hu
        y = jnp.dot(act.astype(BF16), wd_bf[...], preferred_element_type=F32)
        y_ref[...] = _pack_bf16_pairs(y)

    @pl.when(b + 1 < nused)
    def _():
        expert_block(True)

    @pl.when(b + 1 == nused)
    def _():
        expert_block(False)


def _moe(blk_expert, n_used, row_tok, xn, w_gate, w_up, w_down, *, tb):
    d = w_gate.shape[1]
    assert xn.shape[1] * 2 == d and xn.dtype == jnp.uint32
    n_blocks = blk_expert.shape[0]
    de = w_gate.shape[2]
    kern = functools.partial(_moe_kernel, tb=tb)
    grid_spec = pltpu.PrefetchScalarGridSpec(
        num_scalar_prefetch=3,
        grid=(n_blocks,),
        in_specs=[
            pl.BlockSpec(memory_space=pl.ANY),
            pl.BlockSpec((1, d, de), lambda b, be, nu, rt: (be[b], 0, 0)),
            pl.BlockSpec((1, d, de), lambda b, be, nu, rt: (be[b], 0, 0)),
            pl.BlockSpec((1, de, d), lambda b, be, nu, rt: (be[b], 0, 0)),
        ],
        out_specs=pl.BlockSpec((tb, d // 2), lambda b, be, nu, rt: (b, 0)),
        scratch_shapes=[
            pltpu.VMEM((2, tb, d // 2), jnp.uint32),
            pltpu.VMEM((tb, d), BF16),
            pltpu.VMEM((d, de), BF16),
            pltpu.VMEM((d, de), BF16),
            pltpu.VMEM((de, d), BF16),
            pltpu.SemaphoreType.DMA((2,)),
        ],
    )
    return pl.pallas_call(
        kern,
        grid_spec=grid_spec,
        out_shape=jax.ShapeDtypeStruct((n_blocks * tb, d // 2), jnp.uint32),
        compiler_params=pltpu.CompilerParams(
            dimension_semantics=("arbitrary",), vmem_limit_bytes=VMEM_LIMIT),
        name="moe_experts",
    )(blk_expert, n_used, row_tok, xn, w_gate, w_up, w_down)


def _combine_kernel(pos_ref, h_ref, meta_ref, g_ref, ys_hbm, o_ref, ybuf, sem, *, tm, final_norm):
    i = pl.program_id(0)
    nsteps = pl.num_programs(0)

    def row_copy(blk, r, k, slot):
        p = pos_ref[(blk * tm + r) * TOP_K_INNER + k]
        return pltpu.make_async_copy(ys_hbm.at[pl.ds(p, 1)], ybuf.at[slot, k, pl.ds(r, 1)], sem.at[slot])

    def issue(blk, slot):
        def body(r, carry):
            for k in range(TOP_K_INNER):
                row_copy(blk, r, k, slot).start()
            return carry
        lax.fori_loop(0, tm, body, 0, unroll=ISSUE_UNROLL)

    @pl.when(i == 0)
    def _():
        issue(0, 0)

    @pl.when(i + 1 < nsteps)
    def _():
        issue(i + 1, (i + 1) % 2)

    slot = i % 2
    for k in range(TOP_K_INNER):
        pltpu.make_async_copy(ys_hbm.at[pl.ds(0, tm)], ybuf.at[slot, k], sem.at[slot]).wait()

    meta = meta_ref[...]
    half = ybuf.shape[-1]
    h_hi = h_ref[:, :half]
    h_lo = h_ref[:, half:]
    for k in range(TOP_K_INNER):
        gate = meta[:, TOP_K_INNER + k:TOP_K_INNER + k + 1]
        y_hi, y_lo = _unpack_bf16_pairs(ybuf[slot, k])
        h_hi = h_hi + gate * y_hi
        h_lo = h_lo + gate * y_lo
    h = jnp.concatenate([h_hi, h_lo], axis=1)
    o_ref[...] = _rms(h, g_ref[...]) if final_norm else h


def _combine(pos, h, meta, g_final, ys, *, tm, final_norm):
    n, d = h.shape
    assert n % tm == 0
    kern = functools.partial(_combine_kernel, tm=tm, final_norm=final_norm)
    grid_spec = pltpu.PrefetchScalarGridSpec(
        num_scalar_prefetch=1,
        grid=(n // tm,),
        in_specs=[
            pl.BlockSpec((tm, d), lambda i, p: (i, 0)),
            pl.BlockSpec((tm, ROUTER_LANES), lambda i, p: (i, 0)),
            pl.BlockSpec((1, d), lambda i, p: (0, 0)),
            pl.BlockSpec(memory_space=pl.ANY),
        ],
        out_specs=pl.BlockSpec((tm, d), lambda i, p: (i, 0)),
        scratch_shapes=[
            pltpu.VMEM((2, TOP_K_INNER, tm, d // 2), jnp.uint32),
            pltpu.SemaphoreType.DMA((2,)),
        ],
    )
    return pl.pallas_call(
        kern,
        grid_spec=grid_spec,
        out_shape=jax.ShapeDtypeStruct((n, d), F32),
        compiler_params=pltpu.CompilerParams(
            dimension_semantics=("arbitrary",), vmem_limit_bytes=VMEM_LIMIT),
        name="combine",
    )(pos, h, meta, g_final, ys)


def _dispatch_plan(expert_id, n_experts, tb):
    n = expert_id.shape[0]
    a = n * TOP_K_INNER
    assert a % RANK_BLOCK == 0
    flat_e = expert_id.reshape(a)
    onehot = flat_e[:, None] == jnp.arange(n_experts, dtype=jnp.int32)[None, :]
    oh = onehot.astype(BF16).reshape(a // RANK_BLOCK, RANK_BLOCK, n_experts)
    tril = jnp.tril(jnp.ones((RANK_BLOCK, RANK_BLOCK), BF16))
    within = jnp.einsum("ij,bjk->bik", tril, oh, preferred_element_type=F32)
    blk_tot = within[:, -1, :]
    blk_off = jnp.cumsum(blk_tot, axis=0) - blk_tot
    incl = (within + blk_off[:, None, :]).reshape(a, n_experts)
    rank = jnp.sum(jnp.where(onehot, incl, 0.0), axis=1).astype(jnp.int32) - 1
    counts = (blk_off[-1] + blk_tot[-1]).astype(jnp.int32)
    padded = (counts + tb - 1) // tb * tb
    pends = jnp.cumsum(padded)
    pstarts = pends - padded
    pos = jnp.sum(jnp.where(onehot, pstarts[None, :], 0), axis=1).astype(jnp.int32) + rank
    n_blocks = -(-(a + n_experts * (tb - 1)) // tb)
    tok = jnp.arange(a, dtype=jnp.int32) // TOP_K_INNER
    row_tok = jnp.zeros((n_blocks * tb,), jnp.int32).at[pos].set(tok, unique_indices=True)
    blk_start = jnp.arange(n_blocks, dtype=jnp.int32) * tb
    blk_expert = jnp.sum((pends[None, :] <= blk_start[:, None]).astype(jnp.int32), axis=1)
    blk_expert = jnp.minimum(blk_expert, n_experts - 1).astype(jnp.int32)
    n_used = (pends[-1] // tb).astype(jnp.int32).reshape(1)
    return blk_expert, n_used, row_tok, pos


def _pick(limit, total):
    t = min(limit, total)
    assert total % t == 0
    return t


def kernel(x, norm_mix_g, w_in, w_pool, pool_scale, lambda_q1, lambda_k1, lambda_q2, lambda_k2, subln_g, w_out, norm_ffn_g, w_grp, b_grp, w_exp, b_exp, w_gate, w_up, w_down, norm_final_g):
    batch, seq, d = x.shape
    n = batch * seq
    depth = w_in.shape[0]
    pool_width = w_pool.shape[1] * w_pool.shape[2]
    attn_width = (w_in.shape[2] - pool_width) // 3
    n_heads = attn_width // HEAD_WIDTH
    n_groups, per_group = w_exp.shape[1], w_exp.shape[3]
    n_experts = n_groups * per_group
    assert n_groups + n_experts <= ROUTER_LANES

    tm_proj = _pick(512, seq)
    tn_proj = _pick(1024, math.gcd(pool_width, attn_width))
    tq = _pick(1024, seq)
    tm_post = _pick(256, seq)
    tb = 256
    tm_comb = _pick(256, n)

    cs, sn = _rope_tables(seq)
    h = x.reshape(n, d)
    for l in range(depth):
        lam_init = 0.8 - 0.6 * math.exp(-0.3 * l)
        row_width = pool_width + 2 * attn_width
        w_rows = w_in[l][:, :row_width].astype(BF16)
        w_vt = jnp.transpose(w_in[l][:, row_width:]).astype(BF16)
        proj, vt = _inproj(h, norm_mix_g[l][None, :], w_rows, w_vt, cs, sn, seq=seq,
                           pool_width=pool_width, attn_width=attn_width, tm=tm_proj, tn=tn_proj)
        lam_params = jnp.stack([lambda_q1[l], lambda_k1[l], lambda_q2[l], lambda_k2[l]]).astype(F32)
        col = lambda width: width // HEAD_WIDTH
        attn = _attention(proj, vt, lam_params, subln_g[l][:, None].astype(F32), batch=batch, seq=seq,
                          n_heads=n_heads, q_col=col(pool_width), k_col=col(pool_width + attn_width),
                          tq=tq, tk=tq, lam_init=lam_init)

        w_rt = jnp.concatenate([w_grp[l], jnp.transpose(w_exp[l], (1, 0, 2)).reshape(d, n_experts)], axis=1)
        w_rt = jnp.pad(w_rt, ((0, 0), (0, ROUTER_LANES - w_rt.shape[1]))).astype(BF16)
        b_rt = jnp.concatenate([b_grp[l], b_exp[l].reshape(n_experts)]).astype(F32)
        b_rt = jnp.pad(b_rt, (0, ROUTER_LANES - b_rt.shape[0]))[None, :]
        h1, xn, meta = _post(proj, attn, h, w_pool[l].astype(BF16), pool_scale[l][None, :].astype(F32),
                             w_out[l].astype(BF16), norm_ffn_g[l][None, :].astype(F32), w_rt, b_rt,
                             seq=seq, pool_width=pool_width, tm=tm_post, n_groups=n_groups,
                             per_group=per_group)

        expert_id = meta[:, :TOP_K_INNER].astype(jnp.int32)
        blk_expert, n_used, row_tok, pos = _dispatch_plan(expert_id, n_experts, tb)
        ys = _moe(blk_expert, n_used, row_tok, xn, w_gate[l], w_up[l], w_down[l], tb=tb)
        h = _combine(pos, h1, meta, norm_final_g[None, :].astype(F32), ys, tm=tm_comb,
                     final_norm=(l == depth - 1))
    return h.reshape(batch, seq, d)
```

```python
import functools
import math

import jax
import jax.numpy as jnp
from jax import lax
from jax.experimental import pallas as pl
from jax.experimental.pallas import tpu as pltpu

F32 = jnp.float32
BF16 = jnp.bfloat16

EPS = 1e-6
POOL_WINDOWS = (2, 4, 8, 16)
POOL_HALO = 16
HEAD_DIM = 64
HEAD_WIDTH = 2 * HEAD_DIM
ROPE_DIM = HEAD_DIM // 4
ROPE_THETA = 500000.0
Q_SCALE = HEAD_DIM ** -0.5 * math.log2(math.e)
TOP_K_INNER = 2
MASK_VALUE = -1e30
LANES = 128
ROUTER_LANES = 128
ONES_ROWS = 16
ISSUE_UNROLL = 8
RANK_BLOCK = 256
VMEM_LIMIT = 56 * 1024 * 1024


def _rms(x, g):
    ms = jnp.mean(x * x, axis=-1, keepdims=True)
    return x * lax.rsqrt(ms + EPS) * g


def _pack_bf16_pairs(x):
    half = x.shape[1] // 2
    rounded = x.astype(BF16).astype(F32)
    hi_bits = lax.bitcast_convert_type(rounded[:, :half], jnp.uint32)
    lo_bits = lax.bitcast_convert_type(rounded[:, half:], jnp.uint32)
    return hi_bits | (lo_bits >> 16)


def _unpack_bf16_pairs(words):
    hi = lax.bitcast_convert_type(words & jnp.uint32(0xFFFF0000), F32)
    lo = lax.bitcast_convert_type(words << 16, F32)
    return hi, lo


def _inproj_kernel(x_ref, g_ref, w_ref, wvt_ref, cs_ref, sn_ref, o_ref, vt_ref, u_ref, *,
                   first_rope_blk, n_rope_blk, n_row_blk):
    j = pl.program_id(1)

    @pl.when(j == 0)
    def _():
        u_ref[...] = _rms(x_ref[...], g_ref[...]).astype(BF16)

    is_rope = (j >= first_rope_blk) & (j < first_rope_blk + 2 * n_rope_blk)

    @pl.when(j < n_row_blk)
    def _():
        acc = jnp.dot(u_ref[...], w_ref[...], preferred_element_type=F32)

        @pl.when(is_rope)
        def _():
            scale = jnp.where(j < first_rope_blk + n_rope_blk, Q_SCALE, 1.0).astype(F32)
            cs = cs_ref[...]
            sn = sn_ref[...]
            lane = lax.broadcasted_iota(jnp.int32, cs.shape, 1)
            first_half = (lane % HEAD_DIM) < (ROPE_DIM // 2)
            for c in range(acc.shape[1] // LANES):
                t = acc[:, c * LANES:(c + 1) * LANES]
                partner = jnp.where(first_half,
                                    pltpu.roll(t, LANES - ROPE_DIM // 2, 1),
                                    pltpu.roll(t, ROPE_DIM // 2, 1))
                o_ref[:, c * LANES:(c + 1) * LANES] = ((t * cs + partner * sn) * scale).astype(BF16)

        @pl.when(jnp.logical_not(is_rope))
        def _():
            o_ref[...] = acc.astype(BF16)

    @pl.when(j == n_row_blk)
    def _():
        vt = lax.dot_general(wvt_ref[...], u_ref[...], (((1,), (1,)), ((), ())), preferred_element_type=F32)
        vt_ref[...] = vt.astype(BF16)


def _inproj(x2, g, w_rows, w_vt, cs, sn, *, seq, pool_width, attn_width, tm, tn):
    n, d = x2.shape
    width = w_rows.shape[1]
    assert n % tm == 0 and seq % tm == 0 and pool_width % tn == 0 and attn_width % tn == 0
    n_row_blk = width // tn
    kern = functools.partial(_inproj_kernel, first_rope_blk=pool_width // tn, n_rope_blk=attn_width // tn,
                             n_row_blk=n_row_blk)
    seq_blocks = seq // tm
    last = n_row_blk - 1
    return pl.pallas_call(
        kern,
        grid=(n // tm, n_row_blk + 1),
        in_specs=[
            pl.BlockSpec((tm, d), lambda i, j: (i, 0)),
            pl.BlockSpec((1, d), lambda i, j: (0, 0)),
            pl.BlockSpec((d, tn), lambda i, j: (0, jnp.minimum(j, last))),
            pl.BlockSpec(w_vt.shape, lambda i, j: (0, 0)),
            pl.BlockSpec((tm, LANES), lambda i, j: (i % seq_blocks, 0)),
            pl.BlockSpec((tm, LANES), lambda i, j: (i % seq_blocks, 0)),
        ],
        out_specs=[
            pl.BlockSpec((tm, tn), lambda i, j: (i, jnp.minimum(j, last))),
            pl.BlockSpec((attn_width, tm), lambda i, j: (0, i)),
        ],
        out_shape=[
            jax.ShapeDtypeStruct((n, width), BF16),
            jax.ShapeDtypeStruct((attn_width, n), BF16),
        ],
        scratch_shapes=[pltpu.VMEM((tm, d), BF16)],
        compiler_params=pltpu.CompilerParams(
            dimension_semantics=("arbitrary", "arbitrary"), vmem_limit_bytes=VMEM_LIMIT),
        name="inproj",
    )(x2, g, w_rows, w_vt, cs, sn)


def _rope_tables(seq):
    half = ROPE_DIM // 2
    inv = ROPE_THETA ** (-jnp.arange(0, ROPE_DIM, 2, dtype=F32) / ROPE_DIM)
    ang = jnp.arange(seq, dtype=F32)[:, None] * inv[None, :]
    cos, sin = jnp.cos(ang), jnp.sin(ang)
    rest = HEAD_DIM - ROPE_DIM
    cs = jnp.concatenate([cos, cos, jnp.ones((seq, rest), F32)], axis=1)
    sn = jnp.concatenate([-sin, sin, jnp.zeros((seq, rest), F32)], axis=1)
    reps = LANES // HEAD_DIM
    return jnp.tile(cs, (1, reps)), jnp.tile(sn, (1, reps))


def _attn_kernel(lam_ref, g_ref, q_ref, k_ref, vt_ref, o_ref, m_ref, acc_ref,
                 s_a, s_b, mx_a, mx_b, p_a, p_b, al_a, al_b, *, tq, tk, lam_init):
    qi = pl.program_id(2)
    q = q_ref[...]
    lane = lax.broadcasted_iota(jnp.int32, q.shape, 1)
    zero = jnp.zeros_like(q)
    qs = (jnp.where(lane < HEAD_DIM, q, zero), jnp.where(lane >= HEAD_DIM, q, zero))

    m_ref[...] = jnp.full(m_ref.shape, MASK_VALUE, F32)
    acc_ref[...] = jnp.zeros(acc_ref.shape, F32)
    p_b[...] = jnp.zeros(p_b.shape, BF16)
    al_b[...] = jnp.ones(al_b.shape, F32)

    def scores(j, s_ref, mx_ref):
        k = k_ref[pl.ds(pl.multiple_of(j * tk, tk), tk), :]
        for c in range(2):
            s = lax.dot_general(k, qs[c], (((1,), (1,)), ((), ())), preferred_element_type=F32)
            s_ref[c] = s
            mx_ref[c] = jnp.max(s, axis=0, keepdims=True)

    def softmax(s_ref, mx_ref, p_ref, al_ref, masked):
        if masked:
            key = qi * tq + lax.broadcasted_iota(jnp.int32, (tk, tq), 0)
            qry = qi * tq + lax.broadcasted_iota(jnp.int32, (tk, tq), 1)
            keep = key <= qry
        for c in range(2):
            s = s_ref[c]
            if masked:
                s = jnp.where(keep, s, MASK_VALUE)
                mx = jnp.max(s, axis=0, keepdims=True)
            else:
                mx = mx_ref[c]
            m_prev = m_ref[c]
            m_new = jnp.maximum(m_prev, mx)
            al_ref[c] = jnp.exp2(m_prev - m_new)
            p_ref[c] = jnp.exp2(s - m_new).astype(BF16)
            m_ref[c] = m_new

    def accumulate(j, p_ref, al_ref):
        start = pl.multiple_of(jnp.maximum(j, 0) * tk, tk)
        vt = jnp.concatenate([vt_ref[:, pl.ds(start, tk)], jnp.ones((ONES_ROWS, tk), BF16)], axis=0)
        for c in range(2):
            acc_ref[c] = acc_ref[c] * al_ref[c] + jnp.dot(vt, p_ref[c], preferred_element_type=F32)

    def step(j, cur, nxt):
        s_cur, mx_cur, p_cur, al_cur = cur
        s_nxt, mx_nxt, p_nxt, al_nxt = nxt
        scores(j + 1, s_nxt, mx_nxt)
        softmax(s_cur, mx_cur, p_cur, al_cur, False)
        accumulate(j - 1, p_nxt, al_nxt)

    def finish(j, cur, nxt):
        s_cur, mx_cur, p_cur, al_cur = cur
        _, _, p_nxt, al_nxt = nxt
        softmax(s_cur, mx_cur, p_cur, al_cur, True)
        accumulate(j - 1, p_nxt, al_nxt)
        accumulate(j, p_cur, al_cur)

    buf_a = (s_a, mx_a, p_a, al_a)
    buf_b = (s_b, mx_b, p_b, al_b)
    scores(0, s_a, mx_a)

    def pair(jj, carry):
        step(2 * jj, buf_a, buf_b)
        step(2 * jj + 1, buf_b, buf_a)
        return carry

    lax.fori_loop(0, qi // 2, pair, 0)

    @pl.when(qi % 2 == 1)
    def _():
        step(qi - 1, buf_a, buf_b)
        finish(qi, buf_b, buf_a)

    @pl.when(qi % 2 == 0)
    def _():
        finish(qi, buf_a, buf_b)

    lp = lam_ref[...]
    lam = (jnp.exp(jnp.sum(lp[0:1] * lp[1:2], axis=1, keepdims=True))
           - jnp.exp(jnp.sum(lp[2:3] * lp[3:4], axis=1, keepdims=True)) + lam_init)
    num = [acc_ref[c][:HEAD_WIDTH, :] for c in range(2)]
    den = [acc_ref[c][HEAD_WIDTH:HEAD_WIDTH + 1, :] for c in range(2)]
    ot = num[0] / den[0] - lam * (num[1] / den[1])
    ms = jnp.mean(ot * ot, axis=0, keepdims=True)
    ot = ot * lax.rsqrt(ms + EPS) * (g_ref[...] * (1.0 - lam_init))
    o_ref[...] = ot.T.astype(BF16)


def _attention(proj, vt, lam_params, subln_g, *, batch, seq, n_heads, q_col, k_col, tq, tk, lam_init):
    n = batch * seq
    assert seq % tq == 0 and tk == tq
    nq = seq // tq
    stage = lambda shape, dtype: [pltpu.VMEM((2,) + shape, dtype)] * 2
    kern = functools.partial(_attn_kernel, tq=tq, tk=tk, lam_init=lam_init)
    return pl.pallas_call(
        kern,
        grid=(batch, n_heads, nq),
        in_specs=[
            pl.BlockSpec(lam_params.shape, lambda b, h, i: (0, 0)),
            pl.BlockSpec((HEAD_WIDTH, 1), lambda b, h, i: (0, 0)),
            pl.BlockSpec((tq, HEAD_WIDTH), lambda b, h, i: (b * nq + i, q_col + h)),
            pl.BlockSpec((seq, HEAD_WIDTH), lambda b, h, i: (b, k_col + h)),
            pl.BlockSpec((HEAD_WIDTH, seq), lambda b, h, i: (h, b)),
        ],
        out_specs=pl.BlockSpec((tq, HEAD_WIDTH), lambda b, h, i: (b * nq + i, h)),
        out_shape=jax.ShapeDtypeStruct((n, n_heads * HEAD_WIDTH), BF16),
        scratch_shapes=[
            pltpu.VMEM((2, 1, tq), F32),
            pltpu.VMEM((2, HEAD_WIDTH + ONES_ROWS, tq), F32),
            *stage((tk, tq), F32), *stage((1, tq), F32), *stage((tk, tq), BF16), *stage((1, tq), F32),
        ],
        compiler_params=pltpu.CompilerParams(
            dimension_semantics=("arbitrary", "arbitrary", "arbitrary"), vmem_limit_bytes=VMEM_LIMIT),
        name="diff_attention",
    )(lam_params, subln_g, proj, proj, vt)


def _post_kernel(pool_ref, halo_ref, attn_ref, x_ref, wpool_ref, pscale_ref, wout_ref, gffn_ref,
                 wrt_ref, brt_ref, h_ref, xn_ref, meta_ref, *, tm, seq_blocks, n_groups, per_group):
    i = pl.program_id(0)
    blk_in_seq = i % seq_blocks
    halo = halo_ref[...].astype(F32) * jnp.where(blk_in_seq == 0, 0.0, 1.0).astype(F32)
    cur = pool_ref[...].astype(F32)
    ext = jnp.concatenate([halo, cur], axis=0)
    pos = blk_in_seq * tm + lax.broadcasted_iota(jnp.int32, (tm, 1), 0)
    pool_width = cur.shape[1]
    group = pool_width // len(POOL_WINDOWS)

    h = x_ref[...]
    for g, w in enumerate(POOL_WINDOWS):
        sl = slice(g * group, (g + 1) * group)
        s = ext[:, sl]
        span = 1
        while span < w:
            s = s + pltpu.roll(s, span, 0)
            span *= 2
        count = jnp.minimum(pos + 1, w).astype(F32)
        pooled = s[POOL_HALO:, :] / count - cur[:, sl]
        mixed = jnp.dot(pooled.astype(BF16), wpool_ref[g], preferred_element_type=F32) * pscale_ref[:, sl]
        h = h + jnp.dot(mixed.astype(BF16), wout_ref[sl, :], preferred_element_type=F32)
    h = h + jnp.dot(attn_ref[...], wout_ref[pool_width:, :], preferred_element_type=F32)
    h_ref[...] = h

    xn = _rms(h, gffn_ref[...])
    xn_ref[...] = _pack_bf16_pairs(xn)
    logits = jnp.dot(xn.astype(BF16), wrt_ref[...], preferred_element_type=F32) + brt_ref[...]

    lane = lax.broadcasted_iota(jnp.int32, logits.shape, 1).astype(F32)
    big = float(ROUTER_LANES)
    is_grp = lane < n_groups
    gl = jnp.where(is_grp, logits, MASK_VALUE)
    gmax = jnp.max(gl, axis=1, keepdims=True)
    gsum = jnp.sum(jnp.where(is_grp, jnp.exp(gl - gmax), 0.0), axis=1, keepdims=True)
    grp_gate = 1.0 / gsum
    gidx = jnp.min(jnp.where(gl == gmax, lane, big), axis=1, keepdims=True)
    lo = n_groups + gidx * per_group
    in_sel = (lane >= lo) & (lane < lo + per_group)
    el = jnp.where(in_sel, logits, MASK_VALUE)
    e1 = jnp.max(el, axis=1, keepdims=True)
    i1 = jnp.min(jnp.where(el == e1, lane, big), axis=1, keepdims=True)
    el2 = jnp.where(lane == i1, MASK_VALUE, el)
    e2 = jnp.max(el2, axis=1, keepdims=True)
    i2 = jnp.min(jnp.where(el2 == e2, lane, big), axis=1, keepdims=True)
    t = jnp.exp(e2 - e1)
    g1 = grp_gate / (1.0 + t)
    g2 = g1 * t
    meta = jnp.where(lane == 0, i1 - n_groups,
                     jnp.where(lane == 1, i2 - n_groups,
                               jnp.where(lane == 2, g1, jnp.where(lane == 3, g2, 0.0))))
    meta_ref[...] = meta


def _post(proj, attn, x2, w_pool, pool_scale, w_out, g_ffn, w_rt, b_rt, *, seq, pool_width, tm,
          n_groups, per_group):
    n, d = x2.shape
    assert n % tm == 0 and seq % tm == 0 and tm % POOL_HALO == 0
    seq_blocks = seq // tm
    halo_per_blk = tm // POOL_HALO
    kern = functools.partial(_post_kernel, tm=tm, seq_blocks=seq_blocks, n_groups=n_groups, per_group=per_group)
    const2 = lambda i: (0, 0)
    return pl.pallas_call(
        kern,
        grid=(n // tm,),
        in_specs=[
            pl.BlockSpec((tm, pool_width), lambda i: (i, 0)),
            pl.BlockSpec((POOL_HALO, pool_width), lambda i: (jnp.maximum(i * halo_per_blk - 1, 0), 0)),
            pl.BlockSpec((tm, attn.shape[1]), lambda i: (i, 0)),
            pl.BlockSpec((tm, d), lambda i: (i, 0)),
            pl.BlockSpec(w_pool.shape, lambda i: (0, 0, 0)),
            pl.BlockSpec(pool_scale.shape, const2),
            pl.BlockSpec(w_out.shape, const2),
            pl.BlockSpec(g_ffn.shape, const2),
            pl.BlockSpec(w_rt.shape, const2),
            pl.BlockSpec(b_rt.shape, const2),
        ],
        out_specs=[
            pl.BlockSpec((tm, d), lambda i: (i, 0)),
            pl.BlockSpec((tm, d // 2), lambda i: (i, 0)),
            pl.BlockSpec((tm, ROUTER_LANES), lambda i: (i, 0)),
        ],
        out_shape=[
            jax.ShapeDtypeStruct((n, d), F32),
            jax.ShapeDtypeStruct((n, d // 2), jnp.uint32),
            jax.ShapeDtypeStruct((n, ROUTER_LANES), F32),
        ],
        compiler_params=pltpu.CompilerParams(
            dimension_semantics=("arbitrary",), vmem_limit_bytes=VMEM_LIMIT),
        name="post_mix_router",
    )(proj, proj, attn, x2, w_pool, pool_scale, w_out, g_ffn, w_rt, b_rt)


def _moe_kernel(be_ref, nused_ref, rtok_ref, xn_hbm, wg_ref, wu_ref, wd_ref, y_ref,
                xbuf, xb_ref, wg_bf, wu_bf, wd_bf, sem, *, tb):
    b = pl.program_id(0)
    nused = nused_ref[0]

    def row_copy(blk, r, slot):
        tok = rtok_ref[blk * tb + r]
        return pltpu.make_async_copy(xn_hbm.at[pl.ds(tok, 1)], xbuf.at[slot, pl.ds(r, 1)], sem.at[slot])

    def issue(blk, slot):
        def body(r, carry):
            row_copy(blk, r, slot).start()
            return carry
        lax.fori_loop(0, tb, body, 0, unroll=ISSUE_UNROLL)

    @pl.when(b == 0)
    def _():
        issue(0, 0)

    @pl.when(b >= nused)
    def _():
        y_ref[...] = jnp.zeros(y_ref.shape, jnp.uint32)

    def expert_block(prefetch_next):
        slot = b % 2
        pltpu.make_async_copy(xn_hbm.at[pl.ds(0, tb)], xbuf.at[slot], sem.at[slot]).wait()
        hi, lo = _unpack_bf16_pairs(xbuf[slot])
        half = hi.shape[1]
        xb_ref[:, :half] = hi.astype(BF16)
        xb_ref[:, half:] = lo.astype(BF16)

        new_expert = (b == 0) | (be_ref[b] != be_ref[jnp.maximum(b - 1, 0)])

        @pl.when(new_expert)
        def _():
            wg_bf[...] = wg_ref[0].astype(BF16)
            wu_bf[...] = wu_ref[0].astype(BF16)
            wd_bf[...] = wd_ref[0].astype(BF16)

        if prefetch_next:
            for r in range(tb):
                row_copy(b + 1, r, 1 - slot).start()
        xb = xb_ref[...]
        hg = jnp.dot(xb, wg_bf[...], preferred_element_type=F32)
        hu = jnp.dot(xb, wu_bf[...], preferred_element_type=F32)
        act = hg * jax.nn.sigmoid(hg) * hu
        y = jnp.dot(act.astype(BF16), wd_bf[...], preferred_element_type=F32)
        y_ref[...] = _pack_bf16_pairs(y)

    @pl.when(b + 1 < nused)
    def _():
        expert_block(True)

    @pl.when(b + 1 == nused)
    def _():
        expert_block(False)


def _moe(blk_expert, n_used, row_tok, xn, w_gate, w_up, w_down, *, tb):
    d = w_gate.shape[1]
    assert xn.shape[1] * 2 == d and xn.dtype == jnp.uint32
    n_blocks = blk_expert.shape[0]
    de = w_gate.shape[2]
    kern = functools.partial(_moe_kernel, tb=tb)
    grid_spec = pltpu.PrefetchScalarGridSpec(
        num_scalar_prefetch=3,
        grid=(n_blocks,),
        in_specs=[
            pl.BlockSpec(memory_space=pl.ANY),
            pl.BlockSpec((1, d, de), lambda b, be, nu, rt: (be[b], 0, 0)),
            pl.BlockSpec((1, d, de), lambda b, be, nu, rt: (be[b], 0, 0)),
            pl.BlockSpec((1, de, d), lambda b, be, nu, rt: (be[b], 0, 0)),
        ],
        out_specs=pl.BlockSpec((tb, d // 2), lambda b, be, nu, rt: (b, 0)),
        scratch_shapes=[
            pltpu.VMEM((2, tb, d // 2), jnp.uint32),
            pltpu.VMEM((tb, d), BF16),
            pltpu.VMEM((d, de), BF16),
            pltpu.VMEM((d, de), BF16),
            pltpu.VMEM((de, d), BF16),
            pltpu.SemaphoreType.DMA((2,)),
        ],
    )
    return pl.pallas_call(
        kern,
        grid_spec=grid_spec,
        out_shape=jax.ShapeDtypeStruct((n_blocks * tb, d // 2), jnp.uint32),
        compiler_params=pltpu.CompilerParams(
            dimension_semantics=("arbitrary",), vmem_limit_bytes=VMEM_LIMIT),
        name="moe_experts",
    )(blk_expert, n_used, row_tok, xn, w_gate, w_up, w_down)


def _combine_kernel(pos_ref, h_ref, meta_ref, g_ref, ys_hbm, o_ref, ybuf, sem, *, tm, final_norm):
    i = pl.program_id(0)
    nsteps = pl.num_programs(0)

    def row_copy(blk, r, k, slot):
        p = pos_ref[(blk * tm + r) * TOP_K_INNER + k]
        return pltpu.make_async_copy(ys_hbm.at[pl.ds(p, 1)], ybuf.at[slot, k, pl.ds(r, 1)], sem.at[slot])

    def issue(blk, slot):
        def body(r, carry):
            for k in range(TOP_K_INNER):
                row_copy(blk, r, k, slot).start()
            return carry
        lax.fori_loop(0, tm, body, 0, unroll=ISSUE_UNROLL)

    @pl.when(i == 0)
    def _():
        issue(0, 0)

    @pl.when(i + 1 < nsteps)
    def _():
        issue(i + 1, (i + 1) % 2)

    slot = i % 2
    for k in range(TOP_K_INNER):
        pltpu.make_async_copy(ys_hbm.at[pl.ds(0, tm)], ybuf.at[slot, k], sem.at[slot]).wait()

    meta = meta_ref[...]
    half = ybuf.shape[-1]
    h_hi = h_ref[:, :half]
    h_lo = h_ref[:, half:]
    for k in range(TOP_K_INNER):
        gate = meta[:, TOP_K_INNER + k:TOP_K_INNER + k + 1]
        y_hi, y_lo = _unpack_bf16_pairs(ybuf[slot, k])
        h_hi = h_hi + gate * y_hi
        h_lo = h_lo + gate * y_lo
    h = jnp.concatenate([h_hi, h_lo], axis=1)
    o_ref[...] = _rms(h, g_ref[...]) if final_norm else h


def _combine(pos, h, meta, g_final, ys, *, tm, final_norm):
    n, d = h.shape
    assert n % tm == 0
    kern = functools.partial(_combine_kernel, tm=tm, final_norm=final_norm)
    grid_spec = pltpu.PrefetchScalarGridSpec(
        num_scalar_prefetch=1,
        grid=(n // tm,),
        in_specs=[
            pl.BlockSpec((tm, d), lambda i, p: (i, 0)),
            pl.BlockSpec((tm, ROUTER_LANES), lambda i, p: (i, 0)),
            pl.BlockSpec((1, d), lambda i, p: (0, 0)),
            pl.BlockSpec(memory_space=pl.ANY),
        ],
        out_specs=pl.BlockSpec((tm, d), lambda i, p: (i, 0)),
        scratch_shapes=[
            pltpu.VMEM((2, TOP_K_INNER, tm, d // 2), jnp.uint32),
            pltpu.SemaphoreType.DMA((2,)),
        ],
    )
    return pl.pallas_call(
        kern,
        grid_spec=grid_spec,
        out_shape=jax.ShapeDtypeStruct((n, d), F32),
        compiler_params=pltpu.CompilerParams(
            dimension_semantics=("arbitrary",), vmem_limit_bytes=VMEM_LIMIT),
        name="combine",
    )(pos, h, meta, g_final, ys)


def _dispatch_plan(expert_id, n_experts, tb):
    n = expert_id.shape[0]
    a = n * TOP_K_INNER
    assert a % RANK_BLOCK == 0
    flat_e = expert_id.reshape(a)
    onehot = flat_e[:, None] == jnp.arange(n_experts, dtype=jnp.int32)[None, :]
    oh = onehot.astype(BF16).reshape(a // RANK_BLOCK, RANK_BLOCK, n_experts)
    tril = jnp.tril(jnp.ones((RANK_BLOCK, RANK_BLOCK), BF16))
    within = jnp.einsum("ij,bjk->bik", tril, oh, preferred_element_type=F32)
    blk_tot = within[:, -1, :]
    blk_off = jnp.cumsum(blk_tot, axis=0) - blk_tot
    incl = (within + blk_off[:, None, :]).reshape(a, n_experts)
    rank = jnp.sum(jnp.where(onehot, incl, 0.0), axis=1).astype(jnp.int32) - 1
    counts = (blk_off[-1] + blk_tot[-1]).astype(jnp.int32)
    padded = (counts + tb - 1) // tb * tb
    pends = jnp.cumsum(padded)
    pstarts = pends - padded
    pos = jnp.sum(jnp.where(onehot, pstarts[None, :], 0), axis=1).astype(jnp.int32) + rank
    n_blocks = -(-(a + n_experts * (tb - 1)) // tb)
    tok = jnp.arange(a, dtype=jnp.int32) // TOP_K_INNER
    row_tok = jnp.zeros((n_blocks * tb,), jnp.int32).at[pos].set(tok, unique_indices=True)
    blk_start = jnp.arange(n_blocks, dtype=jnp.int32) * tb
    blk_expert = jnp.sum((pends[None, :] <= blk_start[:, None]).astype(jnp.int32), axis=1)
    blk_expert = jnp.minimum(blk_expert, n_experts - 1).astype(jnp.int32)
    n_used = (pends[-1] // tb).astype(jnp.int32).reshape(1)
    return blk_expert, n_used, row_tok, pos


def _pick(limit, total):
    t = min(limit, total)
    assert total % t == 0
    return t


def kernel(x, norm_mix_g, w_in, w_pool, pool_scale, lambda_q1, lambda_k1, lambda_q2, lambda_k2, subln_g, w_out, norm_ffn_g, w_grp, b_grp, w_exp, b_exp, w_gate, w_up, w_down, norm_final_g):
    batch, seq, d = x.shape
    n = batch * seq
    depth = w_in.shape[0]
    pool_width = w_pool.shape[1] * w_pool.shape[2]
    attn_width = (w_in.shape[2] - pool_width) // 3
    n_heads = attn_width // HEAD_WIDTH
    n_groups, per_group = w_exp.shape[1], w_exp.shape[3]
    n_experts = n_groups * per_group
    assert n_groups + n_experts <= ROUTER_LANES

    tm_proj = _pick(1024, seq)
    tn_proj = _pick(1024, math.gcd(pool_width, attn_width))
    tq = _pick(1024, seq)
    tm_post = _pick(512, seq)
    tb = 256
    tm_comb = _pick(256, n)

    cs, sn = _rope_tables(seq)
    h = x.reshape(n, d)
    for l in range(depth):
        lam_init = 0.8 - 0.6 * math.exp(-0.3 * l)
        row_width = pool_width + 2 * attn_width
        w_rows = w_in[l][:, :row_width].astype(BF16)
        w_vt = jnp.transpose(w_in[l][:, row_width:]).astype(BF16)
        proj, vt = _inproj(h, norm_mix_g[l][None, :], w_rows, w_vt, cs, sn, seq=seq,
                           pool_width=pool_width, attn_width=attn_width, tm=tm_proj, tn=tn_proj)
        lam_params = jnp.stack([lambda_q1[l], lambda_k1[l], lambda_q2[l], lambda_k2[l]]).astype(F32)
        col = lambda width: width // HEAD_WIDTH
        attn = _attention(proj, vt, lam_params, subln_g[l][:, None].astype(F32), batch=batch, seq=seq,
                          n_heads=n_heads, q_col=col(pool_width), k_col=col(pool_width + attn_width),
                          tq=tq, tk=tq, lam_init=lam_init)

        w_rt = jnp.concatenate([w_grp[l], jnp.transpose(w_exp[l], (1, 0, 2)).reshape(d, n_experts)], axis=1)
        w_rt = jnp.pad(w_rt, ((0, 0), (0, ROUTER_LANES - w_rt.shape[1]))).astype(BF16)
        b_rt = jnp.concatenate([b_grp[l], b_exp[l].reshape(n_experts)]).astype(F32)
        b_rt = jnp.pad(b_rt, (0, ROUTER_LANES - b_rt.shape[0]))[None, :]
        h1, xn, meta = _post(proj, attn, h, w_pool[l].astype(BF16), pool_scale[l][None, :].astype(F32),
                             w_out[l].astype(BF16), norm_ffn_g[l][None, :].astype(F32), w_rt, b_rt,
                             seq=seq, pool_width=pool_width, tm=tm_post, n_groups=n_groups,
                             per_group=per_group)

        expert_id = meta[:, :TOP_K_INNER].astype(jnp.int32)
        blk_expert, n_used, row_tok, pos = _dispatch_plan(expert_id, n_experts, tb)
        ys = _moe(blk_expert, n_used, row_tok, xn, w_gate[l], w_up[l], w_down[l], tb=tb)
        h = _combine(pos, h1, meta, norm_final_g[None, :].astype(F32), ys, tm=tm_comb,
                     final_norm=(l == depth - 1))
    return h.reshape(batch, seq, d)
```

```python
import functools
import math

import jax
import jax.numpy as jnp
from jax import lax
from jax.experimental import pallas as pl
from jax.experimental.pallas import tpu as pltpu

F32 = jnp.float32
BF16 = jnp.bfloat16

EPS = 1e-6
POOL_WINDOWS = (2, 4, 8, 16)
POOL_HALO = 16
HEAD_DIM = 64
HEAD_WIDTH = 2 * HEAD_DIM
ROPE_DIM = HEAD_DIM // 4
ROPE_THETA = 500000.0
Q_SCALE = HEAD_DIM ** -0.5 * math.log2(math.e)
TOP_K_INNER = 2
MASK_VALUE = -1e30
LANES = 128
ROUTER_LANES = 128
ONES_ROWS = 16
ISSUE_UNROLL = 8
RANK_BLOCK = 256
VMEM_LIMIT = 56 * 1024 * 1024


def _rms(x, g):
    ms = jnp.mean(x * x, axis=-1, keepdims=True)
    return x * lax.rsqrt(ms + EPS) * g


def _pack_bf16_pairs(x):
    half = x.shape[1] // 2
    rounded = x.astype(BF16).astype(F32)
    hi_bits = lax.bitcast_convert_type(rounded[:, :half], jnp.uint32)
    lo_bits = lax.bitcast_convert_type(rounded[:, half:], jnp.uint32)
    return hi_bits | (lo_bits >> 16)


def _unpack_bf16_pairs(words):
    hi = lax.bitcast_convert_type(words & jnp.uint32(0xFFFF0000), F32)
    lo = lax.bitcast_convert_type(words << 16, F32)
    return hi, lo


def _inproj_kernel(x_ref, g_ref, w_ref, wvt_ref, cs_ref, sn_ref, o_ref, vt_ref, u_ref, *,
                   first_rope_blk, n_rope_blk, n_row_blk):
    j = pl.program_id(1)

    @pl.when(j == 0)
    def _():
        u_ref[...] = _rms(x_ref[...], g_ref[...]).astype(BF16)

    is_rope = (j >= first_rope_blk) & (j < first_rope_blk + 2 * n_rope_blk)

    @pl.when(j < n_row_blk)
    def _():
        acc = jnp.dot(u_ref[...], w_ref[...], preferred_element_type=F32)

        @pl.when(is_rope)
        def _():
            scale = jnp.where(j < first_rope_blk + n_rope_blk, Q_SCALE, 1.0).astype(F32)
            cs = cs_ref[...]
            sn = sn_ref[...]
            lane = lax.broadcasted_iota(jnp.int32, cs.shape, 1)
            first_half = (lane % HEAD_DIM) < (ROPE_DIM // 2)
            for c in range(acc.shape[1] // LANES):
                t = acc[:, c * LANES:(c + 1) * LANES]
                partner = jnp.where(first_half,
                                    pltpu.roll(t, LANES - ROPE_DIM // 2, 1),
                                    pltpu.roll(t, ROPE_DIM // 2, 1))
                o_ref[:, c * LANES:(c + 1) * LANES] = ((t * cs + partner * sn) * scale).astype(BF16)

        @pl.when(jnp.logical_not(is_rope))
        def _():
            o_ref[...] = acc.astype(BF16)

    @pl.when(j == n_row_blk)
    def _():
        vt = lax.dot_general(wvt_ref[...], u_ref[...], (((1,), (1,)), ((), ())), preferred_element_type=F32)
        vt_ref[...] = vt.astype(BF16)


def _inproj(x2, g, w_rows, w_vt, cs, sn, *, seq, pool_width, attn_width, tm, tn):
    n, d = x2.shape
    width = w_rows.shape[1]
    assert n % tm == 0 and seq % tm == 0 and pool_width % tn == 0 and attn_width % tn == 0
    n_row_blk = width // tn
    kern = functools.partial(_inproj_kernel, first_rope_blk=pool_width // tn, n_rope_blk=attn_width // tn,
                             n_row_blk=n_row_blk)
    seq_blocks = seq // tm
    last = n_row_blk - 1
    return pl.pallas_call(
        kern,
        grid=(n // tm, n_row_blk + 1),
        in_specs=[
            pl.BlockSpec((tm, d), lambda i, j: (i, 0)),
            pl.BlockSpec((1, d), lambda i, j: (0, 0)),
            pl.BlockSpec((d, tn), lambda i, j: (0, jnp.minimum(j, last))),
            pl.BlockSpec(w_vt.shape, lambda i, j: (0, 0)),
            pl.BlockSpec((tm, LANES), lambda i, j: (i % seq_blocks, 0)),
            pl.BlockSpec((tm, LANES), lambda i, j: (i % seq_blocks, 0)),
        ],
        out_specs=[
            pl.BlockSpec((tm, tn), lambda i, j: (i, jnp.minimum(j, last))),
            pl.BlockSpec((attn_width, tm), lambda i, j: (0, i)),
        ],
        out_shape=[
            jax.ShapeDtypeStruct((n, width), BF16),
            jax.ShapeDtypeStruct((attn_width, n), BF16),
        ],
        scratch_shapes=[pltpu.VMEM((tm, d), BF16)],
        compiler_params=pltpu.CompilerParams(
            dimension_semantics=("arbitrary", "arbitrary"), vmem_limit_bytes=VMEM_LIMIT),
        name="inproj",
    )(x2, g, w_rows, w_vt, cs, sn)


def _rope_tables(seq):
    half = ROPE_DIM // 2
    inv = ROPE_THETA ** (-jnp.arange(0, ROPE_DIM, 2, dtype=F32) / ROPE_DIM)
    ang = jnp.arange(seq, dtype=F32)[:, None] * inv[None, :]
    cos, sin = jnp.cos(ang), jnp.sin(ang)
    rest = HEAD_DIM - ROPE_DIM
    cs = jnp.concatenate([cos, cos, jnp.ones((seq, rest), F32)], axis=1)
    sn = jnp.concatenate([-sin, sin, jnp.zeros((seq, rest), F32)], axis=1)
    reps = LANES // HEAD_DIM
    return jnp.tile(cs, (1, reps)), jnp.tile(sn, (1, reps))


def _attn_kernel(lam_ref, g_ref, q_ref, k_ref, vt_ref, o_ref, m_ref, acc_ref,
                 s_a, s_b, mx_a, mx_b, p_a, p_b, al_a, al_b, *, tq, tk, lam_init):
    qi = pl.program_id(2)
    q = q_ref[...]
    lane = lax.broadcasted_iota(jnp.int32, q.shape, 1)
    zero = jnp.zeros_like(q)
    qs = (jnp.where(lane < HEAD_DIM, q, zero), jnp.where(lane >= HEAD_DIM, q, zero))

    m_ref[...] = jnp.full(m_ref.shape, MASK_VALUE, F32)
    acc_ref[...] = jnp.zeros(acc_ref.shape, F32)
    p_b[...] = jnp.zeros(p_b.shape, BF16)
    al_b[...] = jnp.ones(al_b.shape, F32)

    def scores(j, s_ref, mx_ref):
        k = k_ref[pl.ds(pl.multiple_of(j * tk, tk), tk), :]
        for c in range(2):
            s = lax.dot_general(k, qs[c], (((1,), (1,)), ((), ())), preferred_element_type=F32)
            s_ref[c] = s
            mx_ref[c] = jnp.max(s, axis=0, keepdims=True)

    def softmax(s_ref, mx_ref, p_ref, al_ref, masked):
        if masked:
            key = qi * tq + lax.broadcasted_iota(jnp.int32, (tk, tq), 0)
            qry = qi * tq + lax.broadcasted_iota(jnp.int32, (tk, tq), 1)
            keep = key <= qry
        for c in range(2):
            s = s_ref[c]
            if masked:
                s = jnp.where(keep, s, MASK_VALUE)
                mx = jnp.max(s, axis=0, keepdims=True)
            else:
                mx = mx_ref[c]
            m_prev = m_ref[c]
            m_new = jnp.maximum(m_prev, mx)
            al_ref[c] = jnp.exp2(m_prev - m_new)
            p_ref[c] = jnp.exp2(s - m_new).astype(BF16)
            m_ref[c] = m_new

    def accumulate(j, p_ref, al_ref):
        start = pl.multiple_of(jnp.maximum(j, 0) * tk, tk)
        vt = jnp.concatenate([vt_ref[:, pl.ds(start, tk)], jnp.ones((ONES_ROWS, tk), BF16)], axis=0)
        for c in range(2):
            acc_ref[c] = acc_ref[c] * al_ref[c] + jnp.dot(vt, p_ref[c], preferred_element_type=F32)

    def step(j, cur, nxt):
        s_cur, mx_cur, p_cur, al_cur = cur
        s_nxt, mx_nxt, p_nxt, al_nxt = nxt
        scores(j + 1, s_nxt, mx_nxt)
        softmax(s_cur, mx_cur, p_cur, al_cur, False)
        accumulate(j - 1, p_nxt, al_nxt)

    def finish(j, cur, nxt):
        s_cur, mx_cur, p_cur, al_cur = cur
        _, _, p_nxt, al_nxt = nxt
        softmax(s_cur, mx_cur, p_cur, al_cur, True)
        accumulate(j - 1, p_nxt, al_nxt)
        accumulate(j, p_cur, al_cur)

    buf_a = (s_a, mx_a, p_a, al_a)
    buf_b = (s_b, mx_b, p_b, al_b)
    scores(0, s_a, mx_a)

    def pair(jj, carry):
        step(2 * jj, buf_a, buf_b)
        step(2 * jj + 1, buf_b, buf_a)
        return carry

    lax.fori_loop(0, qi // 2, pair, 0)

    @pl.when(qi % 2 == 1)
    def _():
        step(qi - 1, buf_a, buf_b)
        finish(qi, buf_b, buf_a)

    @pl.when(qi % 2 == 0)
    def _():
        finish(qi, buf_a, buf_b)

    lp = lam_ref[...]
    lam = (jnp.exp(jnp.sum(lp[0:1] * lp[1:2], axis=1, keepdims=True))
           - jnp.exp(jnp.sum(lp[2:3] * lp[3:4], axis=1, keepdims=True)) + lam_init)
    num = [acc_ref[c][:HEAD_WIDTH, :] for c in range(2)]
    den = [acc_ref[c][HEAD_WIDTH:HEAD_WIDTH + 1, :] for c in range(2)]
    ot = num[0] / den[0] - lam * (num[1] / den[1])
    ms = jnp.mean(ot * ot, axis=0, keepdims=True)
    ot = ot * lax.rsqrt(ms + EPS) * (g_ref[...] * (1.0 - lam_init))
    o_ref[...] = ot.T.astype(BF16)


def _attention(proj, vt, lam_params, subln_g, *, batch, seq, n_heads, q_col, k_col, tq, tk, lam_init):
    n = batch * seq
    assert seq % tq == 0 and tk == tq
    nq = seq // tq
    stage = lambda shape, dtype: [pltpu.VMEM((2,) + shape, dtype)] * 2
    kern = functools.partial(_attn_kernel, tq=tq, tk=tk, lam_init=lam_init)
    return pl.pallas_call(
        kern,
        grid=(batch, n_heads, nq),
        in_specs=[
            pl.BlockSpec(lam_params.shape, lambda b, h, i: (0, 0)),
            pl.BlockSpec((HEAD_WIDTH, 1), lambda b, h, i: (0, 0)),
            pl.BlockSpec((tq, HEAD_WIDTH), lambda b, h, i: (b * nq + i, q_col + h)),
            pl.BlockSpec((seq, HEAD_WIDTH), lambda b, h, i: (b, k_col + h)),
            pl.BlockSpec((HEAD_WIDTH, seq), lambda b, h, i: (h, b)),
        ],
        out_specs=pl.BlockSpec((tq, HEAD_WIDTH), lambda b, h, i: (b * nq + i, h)),
        out_shape=jax.ShapeDtypeStruct((n, n_heads * HEAD_WIDTH), BF16),
        scratch_shapes=[
            pltpu.VMEM((2, 1, tq), F32),
            pltpu.VMEM((2, HEAD_WIDTH + ONES_ROWS, tq), F32),
            *stage((tk, tq), F32), *stage((1, tq), F32), *stage((tk, tq), BF16), *stage((1, tq), F32),
        ],
        compiler_params=pltpu.CompilerParams(
            dimension_semantics=("arbitrary", "arbitrary", "arbitrary"), vmem_limit_bytes=VMEM_LIMIT),
        name="diff_attention",
    )(lam_params, subln_g, proj, proj, vt)


def _post_kernel(pool_ref, halo_ref, attn_ref, x_ref, wpool_ref, pscale_ref, wout_ref, gffn_ref,
                 wrt_ref, brt_ref, h_ref, xn_ref, meta_ref, *, tm, seq_blocks, n_groups, per_group):
    i = pl.program_id(0)
    blk_in_seq = i % seq_blocks
    halo = halo_ref[...].astype(F32) * jnp.where(blk_in_seq == 0, 0.0, 1.0).astype(F32)
    cur = pool_ref[...].astype(F32)
    ext = jnp.concatenate([halo, cur], axis=0)
    pos = blk_in_seq * tm + lax.broadcasted_iota(jnp.int32, (tm, 1), 0)
    pool_width = cur.shape[1]
    group = pool_width // len(POOL_WINDOWS)

    h = x_ref[...]
    for g, w in enumerate(POOL_WINDOWS):
        sl = slice(g * group, (g + 1) * group)
        s = ext[:, sl]
        span = 1
        while span < w:
            s = s + pltpu.roll(s, span, 0)
            span *= 2
        count = jnp.minimum(pos + 1, w).astype(F32)
        pooled = s[POOL_HALO:, :] / count - cur[:, sl]
        mixed = jnp.dot(pooled.astype(BF16), wpool_ref[g], preferred_element_type=F32) * pscale_ref[:, sl]
        h = h + jnp.dot(mixed.astype(BF16), wout_ref[sl, :], preferred_element_type=F32)
    h = h + jnp.dot(attn_ref[...], wout_ref[pool_width:, :], preferred_element_type=F32)
    h_ref[...] = h

    xn = _rms(h, gffn_ref[...])
    xn_ref[...] = _pack_bf16_pairs(xn)
    logits = jnp.dot(xn.astype(BF16), wrt_ref[...], preferred_element_type=F32) + brt_ref[...]

    lane = lax.broadcasted_iota(jnp.int32, logits.shape, 1).astype(F32)
    big = float(ROUTER_LANES)
    is_grp = lane < n_groups
    gl = jnp.where(is_grp, logits, MASK_VALUE)
    gmax = jnp.max(gl, axis=1, keepdims=True)
    gsum = jnp.sum(jnp.where(is_grp, jnp.exp(gl - gmax), 0.0), axis=1, keepdims=True)
    grp_gate = 1.0 / gsum
    gidx = jnp.min(jnp.where(gl == gmax, lane, big), axis=1, keepdims=True)
    lo = n_groups + gidx * per_group
    in_sel = (lane >= lo) & (lane < lo + per_group)
    el = jnp.where(in_sel, logits, MASK_VALUE)
    e1 = jnp.max(el, axis=1, keepdims=True)
    i1 = jnp.min(jnp.where(el == e1, lane, big), axis=1, keepdims=True)
    el2 = jnp.where(lane == i1, MASK_VALUE, el)
    e2 = jnp.max(el2, axis=1, keepdims=True)
    i2 = jnp.min(jnp.where(el2 == e2, lane, big), axis=1, keepdims=True)
    t = jnp.exp(e2 - e1)
    g1 = grp_gate / (1.0 + t)
    g2 = g1 * t
    meta = jnp.where(lane == 0, i1 - n_groups,
                     jnp.where(lane == 1, i2 - n_groups,
                               jnp.where(lane == 2, g1, jnp.where(lane == 3, g2, 0.0))))
    meta_ref[...] = meta


def _post(proj, attn, x2, w_pool, pool_scale, w_out, g_ffn, w_rt, b_rt, *, seq, pool_width, tm,
          n_groups, per_group):
    n, d = x2.shape
    assert n % tm == 0 and seq % tm == 0 and tm % POOL_HALO == 0
    seq_blocks = seq // tm
    halo_per_blk = tm // POOL_HALO
    kern = functools.partial(_post_kernel, tm=tm, seq_blocks=seq_blocks, n_groups=n_groups, per_group=per_group)
    const2 = lambda i: (0, 0)
    return pl.pallas_call(
        kern,
        grid=(n // tm,),
        in_specs=[
            pl.BlockSpec((tm, pool_width), lambda i: (i, 0)),
            pl.BlockSpec((POOL_HALO, pool_width), lambda i: (jnp.maximum(i * halo_per_blk - 1, 0), 0)),
            pl.BlockSpec((tm, attn.shape[1]), lambda i: (i, 0)),
            pl.BlockSpec((tm, d), lambda i: (i, 0)),
            pl.BlockSpec(w_pool.shape, lambda i: (0, 0, 0)),
            pl.BlockSpec(pool_scale.shape, const2),
            pl.BlockSpec(w_out.shape, const2),
            pl.BlockSpec(g_ffn.shape, const2),
            pl.BlockSpec(w_rt.shape, const2),
            pl.BlockSpec(b_rt.shape, const2),
        ],
        out_specs=[
            pl.BlockSpec((tm, d), lambda i: (i, 0)),
            pl.BlockSpec((tm, d // 2), lambda i: (i, 0)),
            pl.BlockSpec((tm, ROUTER_LANES), lambda i: (i, 0)),
        ],
        out_shape=[
            jax.ShapeDtypeStruct((n, d), F32),
            jax.ShapeDtypeStruct((n, d // 2), jnp.uint32),
            jax.ShapeDtypeStruct((n, ROUTER_LANES), F32),
        ],
        compiler_params=pltpu.CompilerParams(
            dimension_semantics=("arbitrary",), vmem_limit_bytes=VMEM_LIMIT),
        name="post_mix_router",
    )(proj, proj, attn, x2, w_pool, pool_scale, w_out, g_ffn, w_rt, b_rt)


def _moe_kernel(be_ref, nused_ref, rtok_ref, xn_hbm, wg_ref, wu_ref, wd_ref, y_ref,
                xbuf, xb_ref, wg_bf, wu_bf, wd_bf, sem, *, tb):
    b = pl.program_id(0)
    nused = nused_ref[0]

    def row_copy(blk, r, slot):
        tok = rtok_ref[blk * tb + r]
        return pltpu.make_async_copy(xn_hbm.at[pl.ds(tok, 1)], xbuf.at[slot, pl.ds(r, 1)], sem.at[slot])

    def issue(blk, slot):
        def body(r, carry):
            row_copy(blk, r, slot).start()
            return carry
        lax.fori_loop(0, tb, body, 0, unroll=ISSUE_UNROLL)

    @pl.when(b == 0)
    def _():
        issue(0, 0)

    @pl.when(b >= nused)
    def _():
        y_ref[...] = jnp.zeros(y_ref.shape, jnp.uint32)

    def expert_block(prefetch_next):
        slot = b % 2
        pltpu.make_async_copy(xn_hbm.at[pl.ds(0, tb)], xbuf.at[slot], sem.at[slot]).wait()
        hi, lo = _unpack_bf16_pairs(xbuf[slot])
        half = hi.shape[1]
        xb_ref[:, :half] = hi.astype(BF16)
        xb_ref[:, half:] = lo.astype(BF16)

        new_expert = (b == 0) | (be_ref[b] != be_ref[jnp.maximum(b - 1, 0)])

        @pl.when(new_expert)
        def _():
            wg_bf[...] = wg_ref[0].astype(BF16)
            wu_bf[...] = wu_ref[0].astype(BF16)
            wd_bf[...] = wd_ref[0].astype(BF16)

        if prefetch_next:
            for r in range(tb):
                row_copy(b + 1, r, 1 - slot).start(priority=r % 2)
        xb = xb_ref[...]
        hg = jnp.dot(xb, wg_bf[...], preferred_element_type=F32)
        hu = jnp.dot(xb, wu_bf[...], preferred_element_type=F32)
        act = hg * jax.nn.sigmoid(hg) * hu
        y = jnp.dot(act.astype(BF16), wd_bf[...], preferred_element_type=F32)
        y_ref[...] = _pack_bf16_pairs(y)

    @pl.when(b + 1 < nused)
    def _():
        expert_block(True)

    @pl.when(b + 1 == nused)
    def _():
        expert_block(False)


def _moe(blk_expert, n_used, row_tok, xn, w_gate, w_up, w_down, *, tb):
    d = w_gate.shape[1]
    assert xn.shape[1] * 2 == d and xn.dtype == jnp.uint32
    n_blocks = blk_expert.shape[0]
    de = w_gate.shape[2]
    kern = functools.partial(_moe_kernel, tb=tb)
    grid_spec = pltpu.PrefetchScalarGridSpec(
        num_scalar_prefetch=3,
        grid=(n_blocks,),
        in_specs=[
            pl.BlockSpec(memory_space=pl.ANY),
            pl.BlockSpec((1, d, de), lambda b, be, nu, rt: (be[b], 0, 0)),
            pl.BlockSpec((1, d, de), lambda b, be, nu, rt: (be[b], 0, 0)),
            pl.BlockSpec((1, de, d), lambda b, be, nu, rt: (be[b], 0, 0)),
        ],
        out_specs=pl.BlockSpec((tb, d // 2), lambda b, be, nu, rt: (b, 0)),
        scratch_shapes=[
            pltpu.VMEM((2, tb, d // 2), jnp.uint32),
            pltpu.VMEM((tb, d), BF16),
            pltpu.VMEM((d, de), BF16),
            pltpu.VMEM((d, de), BF16),
            pltpu.VMEM((de, d), BF16),
            pltpu.SemaphoreType.DMA((2,)),
        ],
    )
    return pl.pallas_call(
        kern,
        grid_spec=grid_spec,
        out_shape=jax.ShapeDtypeStruct((n_blocks * tb, d // 2), jnp.uint32),
        compiler_params=pltpu.CompilerParams(
            dimension_semantics=("arbitrary",), vmem_limit_bytes=VMEM_LIMIT),
        name="moe_experts",
    )(blk_expert, n_used, row_tok, xn, w_gate, w_up, w_down)


def _combine_kernel(pos_ref, h_ref, meta_ref, g_ref, ys_hbm, o_ref, ybuf, sem, *, tm, final_norm):
    i = pl.program_id(0)
    nsteps = pl.num_programs(0)

    def row_copy(blk, r, k, slot):
        p = pos_ref[(blk * tm + r) * TOP_K_INNER + k]
        return pltpu.make_async_copy(ys_hbm.at[pl.ds(p, 1)], ybuf.at[slot, k, pl.ds(r, 1)], sem.at[slot])

    def issue(blk, slot):
        def body(r, carry):
            for k in range(TOP_K_INNER):
                row_copy(blk, r, k, slot).start(priority=k % 2)
            return carry
        lax.fori_loop(0, tm, body, 0, unroll=ISSUE_UNROLL)

    @pl.when(i == 0)
    def _():
        issue(0, 0)

    @pl.when(i + 1 < nsteps)
    def _():
        issue(i + 1, (i + 1) % 2)

    slot = i % 2
    for k in range(TOP_K_INNER):
        pltpu.make_async_copy(ys_hbm.at[pl.ds(0, tm)], ybuf.at[slot, k], sem.at[slot]).wait()

    meta = meta_ref[...]
    half = ybuf.shape[-1]
    h_hi = h_ref[:, :half]
    h_lo = h_ref[:, half:]
    for k in range(TOP_K_INNER):
        gate = meta[:, TOP_K_INNER + k:TOP_K_INNER + k + 1]
        y_hi, y_lo = _unpack_bf16_pairs(ybuf[slot, k])
        h_hi = h_hi + gate * y_hi
        h_lo = h_lo + gate * y_lo
    h = jnp.concatenate([h_hi, h_lo], axis=1)
    o_ref[...] = _rms(h, g_ref[...]) if final_norm else h


def _combine(pos, h, meta, g_final, ys, *, tm, final_norm):
    n, d = h.shape
    assert n % tm == 0
    kern = functools.partial(_combine_kernel, tm=tm, final_norm=final_norm)
    grid_spec = pltpu.PrefetchScalarGridSpec(
        num_scalar_prefetch=1,
        grid=(n // tm,),
        in_specs=[
            pl.BlockSpec((tm, d), lambda i, p: (i, 0)),
            pl.BlockSpec((tm, ROUTER_LANES), lambda i, p: (i, 0)),
            pl.BlockSpec((1, d), lambda i, p: (0, 0)),
            pl.BlockSpec(memory_space=pl.ANY),
        ],
        out_specs=pl.BlockSpec((tm, d), lambda i, p: (i, 0)),
        scratch_shapes=[
            pltpu.VMEM((2, TOP_K_INNER, tm, d // 2), jnp.uint32),
            pltpu.SemaphoreType.DMA((2,)),
        ],
    )
    return pl.pallas_call(
        kern,
        grid_spec=grid_spec,
        out_shape=jax.ShapeDtypeStruct((n, d), F32),
        compiler_params=pltpu.CompilerParams(
            dimension_semantics=("arbitrary",), vmem_limit_bytes=VMEM_LIMIT),
        name="combine",
    )(pos, h, meta, g_final, ys)


def _dispatch_plan(expert_id, n_experts, tb):
    n = expert_id.shape[0]
    a = n * TOP_K_INNER
    assert a % RANK_BLOCK == 0
    flat_e = expert_id.reshape(a)
    onehot = flat_e[:, None] == jnp.arange(n_experts, dtype=jnp.int32)[None, :]
    oh = onehot.astype(BF16).reshape(a // RANK_BLOCK, RANK_BLOCK, n_experts)
    tril = jnp.tril(jnp.ones((RANK_BLOCK, RANK_BLOCK), BF16))
    within = jnp.einsum("ij,bjk->bik", tril, oh, preferred_element_type=F32)
    blk_tot = within[:, -1, :]
    blk_off = jnp.cumsum(blk_tot, axis=0) - blk_tot
    incl = (within + blk_off[:, None, :]).reshape(a, n_experts)
    rank = jnp.sum(jnp.where(onehot, incl, 0.0), axis=1).astype(jnp.int32) - 1
    counts = (blk_off[-1] + blk_tot[-1]).astype(jnp.int32)
    padded = (counts + tb - 1) // tb * tb
    pends = jnp.cumsum(padded)
    pstarts = pends - padded
    pos = jnp.sum(jnp.where(onehot, pstarts[None, :], 0), axis=1).astype(jnp.int32) + rank
    n_blocks = -(-(a + n_experts * (tb - 1)) // tb)
    tok = jnp.arange(a, dtype=jnp.int32) // TOP_K_INNER
    row_tok = jnp.zeros((n_blocks * tb,), jnp.int32).at[pos].set(tok, unique_indices=True)
    blk_start = jnp.arange(n_blocks, dtype=jnp.int32) * tb
    blk_expert = jnp.sum((pends[None, :] <= blk_start[:, None]).astype(jnp.int32), axis=1)
    blk_expert = jnp.minimum(blk_expert, n_experts - 1).astype(jnp.int32)
    n_used = (pends[-1] // tb).astype(jnp.int32).reshape(1)
    return blk_expert, n_used, row_tok, pos


def _pick(limit, total):
    t = min(limit, total)
    assert total % t == 0
    return t


def kernel(x, norm_mix_g, w_in, w_pool, pool_scale, lambda_q1, lambda_k1, lambda_q2, lambda_k2, subln_g, w_out, norm_ffn_g, w_grp, b_grp, w_exp, b_exp, w_gate, w_up, w_down, norm_final_g):
    batch, seq, d = x.shape
    n = batch * seq
    depth = w_in.shape[0]
    pool_width = w_pool.shape[1] * w_pool.shape[2]
    attn_width = (w_in.shape[2] - pool_width) // 3
    n_heads = attn_width // HEAD_WIDTH
    n_groups, per_group = w_exp.shape[1], w_exp.shape[3]
    n_experts = n_groups * per_group
    assert n_groups + n_experts <= ROUTER_LANES

    tm_proj = _pick(1024, seq)
    tn_proj = _pick(1024, math.gcd(pool_width, attn_width))
    tq = _pick(1024, seq)
    tm_post = _pick(512, seq)
    tb = 256
    tm_comb = _pick(256, n)

    cs, sn = _rope_tables(seq)
    h = x.reshape(n, d)
    for l in range(depth):
        lam_init = 0.8 - 0.6 * math.exp(-0.3 * l)
        row_width = pool_width + 2 * attn_width
        w_rows = w_in[l][:, :row_width].astype(BF16)
        w_vt = jnp.transpose(w_in[l][:, row_width:]).astype(BF16)
        proj, vt = _inproj(h, norm_mix_g[l][None, :], w_rows, w_vt, cs, sn, seq=seq,
                           pool_width=pool_width, attn_width=attn_width, tm=tm_proj, tn=tn_proj)
        lam_params = jnp.stack([lambda_q1[l], lambda_k1[l], lambda_q2[l], lambda_k2[l]]).astype(F32)
        col = lambda width: width // HEAD_WIDTH
        attn = _attention(proj, vt, lam_params, subln_g[l][:, None].astype(F32), batch=batch, seq=seq,
                          n_heads=n_heads, q_col=col(pool_width), k_col=col(pool_width + attn_width),
                          tq=tq, tk=tq, lam_init=lam_init)

        w_rt = jnp.concatenate([w_grp[l], jnp.transpose(w_exp[l], (1, 0, 2)).reshape(d, n_experts)], axis=1)
        w_rt = jnp.pad(w_rt, ((0, 0), (0, ROUTER_LANES - w_rt.shape[1]))).astype(BF16)
        b_rt = jnp.concatenate([b_grp[l], b_exp[l].reshape(n_experts)]).astype(F32)
        b_rt = jnp.pad(b_rt, (0, ROUTER_LANES - b_rt.shape[0]))[None, :]
        h1, xn, meta = _post(proj, attn, h, w_pool[l].astype(BF16), pool_scale[l][None, :].astype(F32),
                             w_out[l].astype(BF16), norm_ffn_g[l][None, :].astype(F32), w_rt, b_rt,
                             seq=seq, pool_width=pool_width, tm=tm_post, n_groups=n_groups,
                             per_group=per_group)

        expert_id = meta[:, :TOP_K_INNER].astype(jnp.int32)
        blk_expert, n_used, row_tok, pos = _dispatch_plan(expert_id, n_experts, tb)
        ys = _moe(blk_expert, n_used, row_tok, xn, w_gate[l], w_up[l], w_down[l], tb=tb)
        h = _combine(pos, h1, meta, norm_final_g[None, :].astype(F32), ys, tm=tm_comb,
                     final_norm=(l == depth - 1))
    return h.reshape(batch, seq, d)
```
